```python
import math
import jax, jax.numpy as jnp
from jax import lax
import numpy as np

D_MODEL = 1024
BATCH = 8
SEQ = 2048
DEPTH = 1
DEC_BATCH = 128
DEC_SEQ = 8
PAST_LEN = 16384
PAGE_SIZE = 128

MIX_WIDTH = D_MODEL
GLA_HEADS = 4
GLA_DV = (MIX_WIDTH // 2) // GLA_HEADS
GLA_DK = GLA_DV // 2
GLA_QK = GLA_HEADS * GLA_DK
GLA_V = GLA_HEADS * GLA_DV
GLA_GATE_RANK = 16
GLA_GATE_TAU = 16.0
GDN_HEADS = 4
GDN_DK = (MIX_WIDTH // 2) // GDN_HEADS
GDN_DV = GDN_DK
GDN_QK = GDN_HEADS * GDN_DK
GDN_V = GDN_HEADS * GDN_DV
GDN_CONV_DIM = 2 * GDN_QK + GDN_V
CONV_W = 4
IN_SIZES = (GLA_QK, GLA_QK, GLA_V, GLA_V, GLA_GATE_RANK, GDN_CONV_DIM, GDN_V, GDN_HEADS, GDN_HEADS)
IN_DIM = sum(IN_SIZES)
D_FF = -(-8 * D_MODEL // (3 * 256)) * 256
CHUNK = 64
NORM_EPS = 1e-5
L2_EPS = 1e-6
DN_ALPHA = (2 * DEPTH) ** 0.25
DN_BETA = (8 * DEPTH) ** -0.25

kernel_name = "hybrid_gla_gdn_deepnorm_step"


def _layer_norm(x, g, b):
    x32 = x.astype(jnp.float32)
    mu = jnp.mean(x32, -1, keepdims=True)
    var = jnp.mean(jnp.square(x32 - mu), -1, keepdims=True)
    return ((x32 - mu) * lax.rsqrt(var + NORM_EPS) * g + b).astype(x.dtype)


def _rms_norm(x, g):
    x32 = x.astype(jnp.float32)
    return x32 * lax.rsqrt(jnp.mean(jnp.square(x32), -1, keepdims=True) + NORM_EPS) * g.astype(jnp.float32)


def _l2_norm(x):
    return x * lax.rsqrt(jnp.sum(jnp.square(x), -1, keepdims=True) + L2_EPS)


def _to_chunks(a, c, n):
    b_, t_, h_, d_ = a.shape
    a = jnp.pad(a, ((0, 0), (0, n * c - t_), (0, 0), (0, 0)))
    return a.reshape(b_, n, c, h_, d_).transpose(1, 0, 3, 2, 4)


def _from_chunks(o, t_):
    n, b_, h_, c, d_ = o.shape
    return o.transpose(1, 0, 3, 2, 4).reshape(b_, n * c, h_, d_)[:, :t_]


def gla_chunked(q, k, v, log_a, s0):
    t_ = q.shape[1]
    c = min(CHUNK, t_)
    n = -(-t_ // c)
    qc, kc, vc, gc = (_to_chunks(a, c, n) for a in (q, k, v, log_a))
    causal = jnp.tril(jnp.ones((c, c), bool))

    def step(s, inp):
        qi, ki, vi, gi = inp
        b = jnp.cumsum(gi, axis=2)
        diff = b[:, :, :, None, :] - b[:, :, None, :, :]
        decay = jnp.exp(jnp.where(causal[:, :, None], diff, -jnp.inf))
        attn = jnp.einsum('bhid,bhjd,bhijd->bhij', qi, ki, decay)
        o = (jnp.einsum('bhij,bhjv->bhiv', attn, vi)
             + jnp.einsum('bhid,bhdv->bhiv', qi * jnp.exp(b), s))
        b_last = b[:, :, -1:, :]
        s = (jnp.exp(b_last[:, :, 0, :])[..., None] * s
             + jnp.einsum('bhjd,bhjv->bhdv', ki * jnp.exp(b_last - b), vi))
        return s, o

    s, o = lax.scan(step, s0, (qc, kc, vc, gc))
    return _from_chunks(o, t_), s


def gdn_chunked(q, k, v, g, beta, s0):
    t_ = q.shape[1]
    dv = v.shape[-1]
    c = min(CHUNK, t_)
    n = -(-t_ // c)
    qc, kc, vc = (_to_chunks(a, c, n) for a in (q, k, v))
    gc, bc = (_to_chunks(a[..., None], c, n)[..., 0] for a in (g, beta))
    causal = jnp.tril(jnp.ones((c, c), bool))
    strict = jnp.tril(jnp.ones((c, c), bool), -1)
    eye = jnp.eye(c, dtype=jnp.float32)

    def step(s, inp):
        qi, ki, vi, gi, bi = inp
        gcum = jnp.cumsum(gi, -1)
        gamma = jnp.exp(jnp.where(causal, gcum[..., :, None] - gcum[..., None, :], -jnp.inf))
        kk = jnp.einsum('bhid,bhjd->bhij', ki, ki)
        a_mat = eye + jnp.where(strict, bi[..., :, None] * kk * gamma, 0.0)
        rhs = jnp.concatenate([vi * bi[..., None], ki * (bi * jnp.exp(gcum))[..., None]], -1)
        sol = lax.linalg.triangular_solve(a_mat, rhs, left_side=True, lower=True, unit_diagonal=True)
        u, w = sol[..., :dv], sol[..., dv:]
        delta = u - jnp.einsum('bhid,bhdv->bhiv', w, s)
        qk = jnp.einsum('bhid,bhjd->bhij', qi, ki) * gamma
        o = (jnp.einsum('bhid,bhdv->bhiv', qi * jnp.exp(gcum)[..., None], s)
             + jnp.einsum('bhij,bhjv->bhiv', qk, delta))
        s = (jnp.exp(gcum[..., -1])[..., None, None] * s
             + jnp.einsum('bhjd,bhjv->bhdv', ki * jnp.exp(gcum[..., -1:] - gcum)[..., None], delta))
        return s, o

    s, o = lax.scan(step, s0, (qc, kc, vc, gc, bc))
    return _from_chunks(o, t_), s


def hybrid_mixer(x, s_gla, s_gdn, conv_buf, w_in, gla_w_gate_up, gla_b_gate, gla_norm_g,
                 gdn_conv_w, gdn_a_log, gdn_dt_bias, gdn_norm_g, w_out):
    f32 = jnp.float32
    bsz, t_, _ = x.shape
    h = jnp.einsum('btd,de->bte', x, w_in).astype(f32)
    parts = []
    off = 0
    for n_cols in IN_SIZES:
        parts.append(h[..., off:off + n_cols])
        off += n_cols
    gq, gk, gv, gg, ga, dqkv, dg, da, db = parts

    q = gq.reshape(bsz, t_, GLA_HEADS, GLA_DK) * GLA_DK ** -0.5
    k = gk.reshape(bsz, t_, GLA_HEADS, GLA_DK)
    v = gv.reshape(bsz, t_, GLA_HEADS, GLA_DV)
    z = jnp.einsum('btr,re->bte', ga, gla_w_gate_up.astype(f32)) + gla_b_gate.astype(f32)
    log_a = (jax.nn.log_sigmoid(z) / GLA_GATE_TAU).reshape(bsz, t_, GLA_HEADS, GLA_DK)
    o_gla, s_gla_new = gla_chunked(q, k, v, log_a, s_gla.astype(f32))
    o_gla = _rms_norm(o_gla, gla_norm_g) * jax.nn.silu(gg.reshape(bsz, t_, GLA_HEADS, GLA_DV))

    xc = jnp.concatenate([conv_buf.astype(f32), dqkv], axis=1)
    cw = gdn_conv_w.astype(f32)
    conv = xc[:, 0:t_] * cw[0]
    for i in range(1, CONV_W):
        conv = conv + xc[:, i:i + t_] * cw[i]
    conv = jax.nn.silu(conv)
    new_buf = xc[:, -(CONV_W - 1):]
    dq = _l2_norm(conv[..., :GDN_QK].reshape(bsz, t_, GDN_HEADS, GDN_DK)) * GDN_DK ** -0.5
    dk = _l2_norm(conv[..., GDN_QK:2 * GDN_QK].reshape(bsz, t_, GDN_HEADS, GDN_DK))
    dvv = conv[..., 2 * GDN_QK:].reshape(bsz, t_, GDN_HEADS, GDN_DV)
    g = -jnp.exp(gdn_a_log.astype(f32)) * jax.nn.softplus(da + gdn_dt_bias.astype(f32))
    beta = jax.nn.sigmoid(db)
    o_gdn, s_gdn_new = gdn_chunked(dq, dk, dvv, g, beta, s_gdn.astype(f32))
    o_gdn = _rms_norm(o_gdn, gdn_norm_g) * jax.nn.silu(dg.reshape(bsz, t_, GDN_HEADS, GDN_DV))

    o = jnp.concatenate([o_gla.reshape(bsz, t_, GLA_V), o_gdn.reshape(bsz, t_, GDN_V)], -1).astype(x.dtype)
    out = jnp.einsum('bte,ed->btd', o, w_out)
    dt_s = s_gla.dtype
    return out, s_gla_new.astype(dt_s), s_gdn_new.astype(dt_s), new_buf.astype(conv_buf.dtype)


def trunk_layer(x, s_gla, s_gdn, conv_buf, w_in, gla_w_gate_up, gla_b_gate, gla_norm_g,
                gdn_conv_w, gdn_a_log, gdn_dt_bias, gdn_norm_g, w_out, ln1_g, ln1_b,
                w_ffn_gate, w_ffn_up, w_ffn_down, ln2_g, ln2_b):
    m, s_gla_new, s_gdn_new, buf_new = hybrid_mixer(
        x, s_gla, s_gdn, conv_buf, w_in, gla_w_gate_up, gla_b_gate, gla_norm_g,
        gdn_conv_w, gdn_a_log, gdn_dt_bias, gdn_norm_g, w_out)
    x = _layer_norm(DN_ALPHA * x + m, ln1_g, ln1_b)
    hid = jax.nn.silu(jnp.einsum('btd,df->btf', x, w_ffn_gate)) * jnp.einsum('btd,df->btf', x, w_ffn_up)
    f = jnp.einsum('btf,fd->btd', hid, w_ffn_down)
    x = _layer_norm(DN_ALPHA * x + f, ln2_g, ln2_b)
    return x, s_gla_new, s_gdn_new, buf_new


def setup_inputs(seed: int = 0) -> dict:
    key = jax.random.key(seed)
    ks = jax.random.split(key, 24)
    f32 = jnp.float32
    L = DEPTH

    def nrm(k, shape, s):
        return jax.random.normal(k, shape, f32) * s

    dt = jnp.exp(jax.random.uniform(ks[11], (L, GDN_HEADS), f32, math.log(1e-3), math.log(1e-1)))
    return {
        "x_prompt": nrm(ks[0], (BATCH, SEQ, D_MODEL), 1.0),
        "x_sample": nrm(ks[1], (DEC_BATCH, DEC_SEQ, D_MODEL), 1.0),
        "state_gla": nrm(ks[2], (L, DEC_BATCH, GLA_HEADS, GLA_DK, GLA_DV), 0.5),
        "state_gdn": nrm(ks[3], (L, DEC_BATCH, GDN_HEADS, GDN_DK, GDN_DV), 0.5),
        "state_gdn_conv": nrm(ks[4], (L, DEC_BATCH, CONV_W - 1, GDN_CONV_DIM), 1.0),
        "w_in": nrm(ks[5], (L, D_MODEL, IN_DIM), D_MODEL ** -0.5),
        "gla_w_gate_up": nrm(ks[6], (L, GLA_GATE_RANK, GLA_QK), GLA_GATE_RANK ** -0.5),
        "gla_b_gate": nrm(ks[7], (L, GLA_QK), 0.1),
        "gla_norm_g": 1.0 + nrm(ks[8], (L, GLA_DV), 0.02),
        "gdn_conv_w": nrm(ks[9], (L, CONV_W, GDN_CONV_DIM), CONV_W ** -0.5),
        "gdn_a_log": jnp.log(jax.random.uniform(ks[10], (L, GDN_HEADS), f32, 1.0, 16.0)),
        "gdn_dt_bias": dt + jnp.log(-jnp.expm1(-dt)),
        "gdn_norm_g": 1.0 + nrm(ks[12], (L, GDN_DV), 0.02),
        "w_out": nrm(ks[13], (L, MIX_WIDTH, D_MODEL), DN_BETA * MIX_WIDTH ** -0.5),
        "ln1_g": 1.0 + nrm(ks[14], (L, D_MODEL), 0.02),
        "ln1_b": nrm(ks[15], (L, D_MODEL), 0.02),
        "w_ffn_gate": nrm(ks[16], (L, D_MODEL, D_FF), D_MODEL ** -0.5),
        "w_ffn_up": nrm(ks[17], (L, D_MODEL, D_FF), D_MODEL ** -0.5),
        "w_ffn_down": nrm(ks[18], (L, D_FF, D_MODEL), DN_BETA * D_FF ** -0.5),
        "ln2_g": 1.0 + nrm(ks[19], (L, D_MODEL), 0.02),
        "ln2_b": nrm(ks[20], (L, D_MODEL), 0.02),
    }


def reference(x_prompt, x_sample, state_gla, state_gdn, state_gdn_conv, w_in, gla_w_gate_up,
              gla_b_gate, gla_norm_g, gdn_conv_w, gdn_a_log, gdn_dt_bias, gdn_norm_g, w_out,
              ln1_g, ln1_b, w_ffn_gate, w_ffn_up, w_ffn_down, ln2_g, ln2_b):
    bp = x_prompt.shape[0]
    yp, ys = x_prompt, x_sample
    p_gla, p_gdn, p_conv, s_gla, s_gdn, s_conv = [], [], [], [], [], []
    for l in range(DEPTH):
        params = (w_in[l], gla_w_gate_up[l], gla_b_gate[l], gla_norm_g[l], gdn_conv_w[l],
                  gdn_a_log[l], gdn_dt_bias[l], gdn_norm_g[l], w_out[l], ln1_g[l], ln1_b[l],
                  w_ffn_gate[l], w_ffn_up[l], w_ffn_down[l], ln2_g[l], ln2_b[l])
        z_gla = jnp.zeros((bp, GLA_HEADS, GLA_DK, GLA_DV), state_gla.dtype)
        z_gdn = jnp.zeros((bp, GDN_HEADS, GDN_DK, GDN_DV), state_gdn.dtype)
        z_conv = jnp.zeros((bp, CONV_W - 1, GDN_CONV_DIM), state_gdn_conv.dtype)
        yp, a1, a2, a3 = trunk_layer(yp, z_gla, z_gdn, z_conv, *params)
        ys, b1, b2, b3 = trunk_layer(ys, state_gla[l], state_gdn[l], state_gdn_conv[l], *params)
        p_gla.append(a1); p_gdn.append(a2); p_conv.append(a3)
        s_gla.append(b1); s_gdn.append(b2); s_conv.append(b3)
    return (yp, ys, jnp.stack(p_gla), jnp.stack(p_gdn), jnp.stack(p_conv),
            jnp.stack(s_gla), jnp.stack(s_gdn), jnp.stack(s_conv))
```

```python
import functools

import numpy as np
import jax
import jax.numpy as jnp
from jax import lax
from jax.experimental import pallas as pl
from jax.experimental.pallas import tpu as pltpu

F32 = jnp.float32
BF16 = jnp.bfloat16

D_MODEL = 1024
GLA_HEADS, GLA_DK, GLA_DV = 4, 64, 128
GLA_QK = GLA_HEADS * GLA_DK
GLA_V = GLA_HEADS * GLA_DV
GLA_GATE_RANK = 16
GLA_GATE_TAU = 16.0
GDN_HEADS, GDN_DK, GDN_DV = 4, 128, 128
GDN_QK = GDN_HEADS * GDN_DK
GDN_V = GDN_HEADS * GDN_DV
GDN_CONV_DIM = 2 * GDN_QK + GDN_V
CONV_W = 4
D_FF = 2816
NORM_EPS = 1e-5
L2_EPS = 1e-6

GLA_COLS = 2 * GLA_QK + 2 * GLA_V
GDN_COLS = GDN_CONV_DIM + GDN_V
SMALL_COLS = 128
SMALL_A0 = GLA_GATE_RANK
SMALL_B0 = GLA_GATE_RANK + GDN_HEADS

MAX_CHUNK = 64
GLA_SUB = 16
LANES = 128
SUBLANES = 8
VMEM_LIMIT = 56 * 1024 * 1024
FF_TILE = 256


def _mm(a, b):
    return jnp.dot(a.astype(BF16), b.astype(BF16), preferred_element_type=F32)


def _mm_nt(a, b):
    return lax.dot_general(a.astype(BF16), b.astype(BF16), (((1,), (1,)), ((), ())),
                           preferred_element_type=F32)


def _mm_tn(a, b):
    return lax.dot_general(a.astype(BF16), b.astype(BF16), (((0,), (0,)), ((), ())),
                           preferred_element_type=F32)


def _split3(x):
    hi = x.astype(BF16)
    r = x - hi.astype(F32)
    mid = r.astype(BF16)
    lo = (r - mid.astype(F32)).astype(BF16)
    return hi, mid, lo


def _sel_mm(c, x):
    hi, mid, lo = _split3(x)
    d = functools.partial(jnp.dot, preferred_element_type=F32)
    return (d(c, lo) + d(c, mid)) + d(c, hi)


def _mm_sel(x, c):
    hi, mid, lo = _split3(x)
    d = functools.partial(jnp.dot, preferred_element_type=F32)
    return (d(lo, c) + d(mid, c)) + d(hi, c)


def _sigmoid(x):
    return 1.0 / (1.0 + jnp.exp(-x))


def _silu(x):
    return x * _sigmoid(x)


def _softplus(x):
    return jnp.maximum(x, 0.0) + jnp.log1p(jnp.exp(-jnp.abs(x)))


def _log_sigmoid(x):
    return jnp.minimum(x, 0.0) - jnp.log1p(jnp.exp(-jnp.abs(x)))


def _layer_norm(x, g, b):
    mu = jnp.mean(x, axis=-1, keepdims=True)
    xc = x - mu
    var = jnp.mean(xc * xc, axis=-1, keepdims=True)
    return xc * lax.rsqrt(var + NORM_EPS) * g + b


def _rms_gate(o, g, gate):
    ms = jnp.mean(o * o, axis=-1, keepdims=True)
    return o * lax.rsqrt(ms + NORM_EPS) * g * _silu(gate)


def _const_spec(shape):
    nd = len(shape)
    return pl.BlockSpec(shape, lambda *_: (0,) * nd, pipeline_mode=pl.Buffered(1))


def _inproj_kernel(x_ref, wg_ref, wd_ref, ws_ref, og_ref, od_ref, os_ref):
    x = x_ref[...].astype(BF16)
    for w_ref, o_ref in ((wg_ref, og_ref), (wd_ref, od_ref), (ws_ref, os_ref)):
        n = w_ref.shape[1]
        step = min(n, 512)
        for c0 in range(0, n, step):
            o_ref[:, c0:c0 + step] = jnp.dot(x, w_ref[:, c0:c0 + step], preferred_element_type=F32)


def _inproj_call(x2d, w_gla, w_gdn, w_small):
    n_tok = x2d.shape[0]
    tm = 512 if n_tok % 512 == 0 else n_tok
    row = lambda i: (i, 0)
    return pl.pallas_call(
        _inproj_kernel,
        grid=(n_tok // tm,),
        in_specs=[pl.BlockSpec((tm, D_MODEL), row),
                  _const_spec(w_gla.shape), _const_spec(w_gdn.shape), _const_spec(w_small.shape)],
        out_specs=[pl.BlockSpec((tm, GLA_COLS), row), pl.BlockSpec((tm, GDN_COLS), row),
                   pl.BlockSpec((tm, SMALL_COLS), row)],
        out_shape=[jax.ShapeDtypeStruct((n_tok, GLA_COLS), F32),
                   jax.ShapeDtypeStruct((n_tok, GDN_COLS), F32),
                   jax.ShapeDtypeStruct((n_tok, SMALL_COLS), F32)],
        compiler_params=pltpu.CompilerParams(dimension_semantics=("arbitrary",),
                                             vmem_limit_bytes=VMEM_LIMIT),
        name="inproj",
    )(x2d, w_gla, w_gdn, w_small)


def _gla_exponent_matrix(c, sb):
    nb = c // sb
    i = np.arange(c)[:, None]
    t = np.arange(c)[None, :]
    blk = i // sb
    mats = [t <= i, (t > blk * sb) & (t <= i), t > i]
    if nb > 1:
        mats.append((t > i) & (t <= (blk + 1) * sb))
        for m in range(1, nb - 1):
            mats.append((t > (blk - m) * sb) & (t <= blk * sb) & (blk >= m))
    return np.concatenate([m_.astype(np.float32) for m_ in mats], axis=0)


def _gla_kernel(hg_ref, hs_ref, s0_ref, wup_ref, bg_ref, ng_ref, cm_ref, e2_ref,
                o_ref, sout_ref, sbd_ref, eq_ref, *, c, sb, n_chunks):
    nb = c // sb
    t_idx = pl.program_id(1)

    @pl.when(t_idx == 0)
    def _():
        sbd_ref[...] = jnp.zeros_like(sbd_ref)
        for p in range(2):
            sbd_ref[p, 0:GLA_DK, 0:GLA_DV] = s0_ref[0, 2 * p]
            sbd_ref[p, GLA_DK:2 * GLA_DK, GLA_DV:2 * GLA_DV] = s0_ref[0, 2 * p + 1]

    lane = lax.broadcasted_iota(jnp.int32, (1, GLA_QK), 1)
    head_mask = [(lax.shift_right_logical(lane, 6) == h).astype(F32) for h in range(GLA_HEADS)]
    r_bd = lax.shift_right_logical(lax.broadcasted_iota(jnp.int32, (2 * GLA_DK, 2 * GLA_DV), 0), 6)
    c_bd = lax.shift_right_logical(lax.broadcasted_iota(jnp.int32, (2 * GLA_DK, 2 * GLA_DV), 1), 7)
    mask_bd = (r_bd == c_bd).astype(F32)
    eye128 = (lax.broadcasted_iota(jnp.int32, (LANES, LANES), 0)
              == lax.broadcasted_iota(jnp.int32, (LANES, LANES), 1))
    row_in_blk = lax.broadcasted_iota(jnp.int32, (sb, GLA_QK), 0)
    if nb > 1:
        sb_shift = sb.bit_length() - 1
        ri = lax.broadcasted_iota(jnp.int32, (GLA_HEADS * c, c), 0)
        ci = lax.broadcasted_iota(jnp.int32, (GLA_HEADS * c, c), 1)
        blk_diff = (lax.shift_right_logical(jnp.bitwise_and(ri, c - 1), sb_shift)
                    - lax.shift_right_logical(ci, sb_shift))

    for ch in range(n_chunks):
        r0 = ch * c
        hs = hs_ref[0, r0:r0 + c, :]
        z = _mm(hs, wup_ref[...]) + bg_ref[...]
        la = _log_sigmoid(z) * (1.0 / GLA_GATE_TAU)
        ex = _sel_mm(cm_ref[...], la)
        b = ex[0:c]
        eq = ex[c:2 * c]
        es = ex[2 * c:3 * c]
        q = hg_ref[0, r0:r0 + c, 0:GLA_QK] * (GLA_DK ** -0.5)
        k = hg_ref[0, r0:r0 + c, GLA_QK:2 * GLA_QK]
        v = hg_ref[0, r0:r0 + c, 2 * GLA_QK:2 * GLA_QK + GLA_V]
        gate = hg_ref[0, r0:r0 + c, 2 * GLA_QK + GLA_V:GLA_COLS]
        qt = q * jnp.exp(eq)

        o_heads = [None] * GLA_HEADS
        if nb > 1:
            kh = k * jnp.exp(ex[3 * c:4 * c])
            a_off = None
            for d in range(1, nb):
                qd = qt if d == 1 else qt * jnp.exp(ex[(3 + d - 1) * c:(4 + d - 1) * c])
                lhs = jnp.concatenate([qd * head_mask[h] for h in range(GLA_HEADS)], axis=0)
                a_d = jnp.where(blk_diff == d, _mm_nt(lhs, kh), 0.0)
                a_off = a_d if a_off is None else a_off + a_d
            for h in range(GLA_HEADS):
                o_heads[h] = _mm(a_off[h * c:(h + 1) * c], v[:, h * GLA_DV:(h + 1) * GLA_DV])

        eq_ref[...] = eq
        p_rows = []
        for jj in range(sb):
            for bi in range(nb):
                rj = bi * sb + jj
                k_j = hg_ref[0, pl.ds(r0 + rj, 1), GLA_QK:2 * GLA_QK]
                eq_j = eq_ref[pl.ds(rj, 1), :]
                blk = slice(bi * sb, (bi + 1) * sb)
                w = jnp.where(row_in_blk >= jj, jnp.exp(jnp.minimum(eq[blk] - eq_j, 0.0)), 0.0)
                p_rows.append(q[blk] * k_j * w)
        p_all = jnp.concatenate(p_rows, axis=0).astype(BF16)
        coef = jnp.dot(p_all, e2_ref[...], preferred_element_type=F32)
        o_diag = None
        for jj in range(sb):
            v_j = jnp.concatenate(
                [jnp.broadcast_to(hg_ref[0, pl.ds(r0 + bi * sb + jj, 1), 2 * GLA_QK:2 * GLA_QK + GLA_V],
                                  (sb, GLA_V)) for bi in range(nb)], axis=0)
            term = coef[jj * c:(jj + 1) * c] * v_j
            o_diag = term if o_diag is None else o_diag + term

        qhat = q * jnp.exp(b)
        ks = k * jnp.exp(es)
        b_last = b[c - 1:c, :]
        for p in range(2):
            s_old = sbd_ref[p]
            inter = _mm(qhat[:, p * LANES:(p + 1) * LANES], s_old)
            for hh in range(2):
                h = 2 * p + hh
                o_h = inter[:, hh * GLA_DV:(hh + 1) * GLA_DV] + o_diag[:, h * GLA_DV:(h + 1) * GLA_DV]
                if o_heads[h] is not None:
                    o_h = o_h + o_heads[h]
                o_ref[0, r0:r0 + c, h * GLA_DV:(h + 1) * GLA_DV] = _rms_gate(
                    o_h, ng_ref[...], gate[:, h * GLA_DV:(h + 1) * GLA_DV])
            upd = _mm_tn(ks[:, p * LANES:(p + 1) * LANES], v[:, 2 * p * GLA_DV:(2 * p + 2) * GLA_DV])
            bl = jnp.broadcast_to(b_last[:, p * LANES:(p + 1) * LANES], (LANES, LANES))
            decay = jnp.exp(jnp.sum(jnp.where(eye128, bl, 0.0), axis=1, keepdims=True))
            sbd_ref[p] = s_old * decay + upd * mask_bd

    @pl.when(t_idx == pl.num_programs(1) - 1)
    def _():
        for p in range(2):
            sout_ref[0, 2 * p] = sbd_ref[p, 0:GLA_DK, 0:GLA_DV]
            sout_ref[0, 2 * p + 1] = sbd_ref[p, GLA_DK:2 * GLA_DK, GLA_DV:2 * GLA_DV]


def _gla_call(h_gla, h_small, s0, wup_pad, b_gate, norm_g, *, chunks_per_step):
    bsz, t_len, _ = h_gla.shape
    c = min(MAX_CHUNK, t_len)
    sb = min(GLA_SUB, c)
    tb = c * chunks_per_step
    assert t_len % tb == 0 and c % sb == 0
    cm = jnp.asarray(_gla_exponent_matrix(c, sb), BF16)
    e2 = jnp.asarray(np.kron(np.eye(GLA_HEADS, dtype=np.float32),
                             np.ones((GLA_DK, GLA_DV), np.float32)), BF16)
    tok = lambda b, t: (b, t, 0)
    per_b = lambda b, t: (b, 0, 0, 0)
    kern = functools.partial(_gla_kernel, c=c, sb=sb, n_chunks=chunks_per_step)
    return pl.pallas_call(
        kern,
        grid=(bsz, t_len // tb),
        in_specs=[pl.BlockSpec((1, tb, GLA_COLS), tok),
                  pl.BlockSpec((1, tb, SMALL_COLS), tok),
                  pl.BlockSpec((1, GLA_HEADS, GLA_DK, GLA_DV), per_b),
                  _const_spec(wup_pad.shape), _const_spec(b_gate.shape), _const_spec(norm_g.shape),
                  _const_spec(cm.shape), _const_spec(e2.shape)],
        out_specs=[pl.BlockSpec((1, tb, GLA_V), tok),
                   pl.BlockSpec((1, GLA_HEADS, GLA_DK, GLA_DV), per_b)],
        out_shape=[jax.ShapeDtypeStruct((bsz, t_len, GLA_V), F32),
                   jax.ShapeDtypeStruct((bsz, GLA_HEADS, GLA_DK, GLA_DV), F32)],
        scratch_shapes=[pltpu.VMEM((2, 2 * GLA_DK, 2 * GLA_DV), F32),
                        pltpu.VMEM((c, GLA_QK), F32)],
        compiler_params=pltpu.CompilerParams(dimension_semantics=("arbitrary", "arbitrary"),
                                             vmem_limit_bytes=VMEM_LIMIT),
        name="gla",
    )(h_gla, h_small, s0, wup_pad, b_gate, norm_g, cm, e2)


def _gdn_kernel(hd_ref, hs_ref, s0_ref, cb_ref, cw_ref, sel_ref, alog_ref, dtb_ref, ng_ref, tri_ref,
                o_ref, sout_ref, cbout_ref, xwin_ref, *, c, n_chunks):
    t_idx = pl.program_id(1)
    tail = SUBLANES - (CONV_W - 1)

    @pl.when(t_idx == 0)
    def _():
        sout_ref[...] = s0_ref[...]
        xwin_ref[0:SUBLANES, :] = jnp.zeros((SUBLANES, GDN_CONV_DIM), F32)
        xwin_ref[tail:SUBLANES, :] = cb_ref[0]

    ri = lax.broadcasted_iota(jnp.int32, (c, c), 0)
    ci = lax.broadcasted_iota(jnp.int32, (c, c), 1)
    eye = ri == ci
    causal = ri >= ci
    strict = ri > ci
    n_levels = c.bit_length() - 2

    for ch in range(n_chunks):
        r0 = ch * c
        xwin_ref[SUBLANES:SUBLANES + c, :] = hd_ref[0, r0:r0 + c, 0:GDN_CONV_DIM]
        conv = None
        for i in range(CONV_W):
            term = xwin_ref[tail + i:tail + i + c, :] * cw_ref[i:i + 1, :]
            conv = term if conv is None else conv + term
        conv = _silu(conv)
        xwin_ref[0:SUBLANES, :] = xwin_ref[c:c + SUBLANES, :]

        hs = hs_ref[0, r0:r0 + c, :]
        ab = _mm_sel(hs, sel_ref[...])
        g = -jnp.exp(alog_ref[...]) * _softplus(ab[:, 0:GDN_V] + dtb_ref[...])
        beta = _sigmoid(ab[:, GDN_V:2 * GDN_V])
        gc = _sel_mm(tri_ref[...], g)

        for h in range(GDN_HEADS):
            hl = slice(h * LANES, (h + 1) * LANES)
            qh = conv[:, h * GDN_DK:(h + 1) * GDN_DK]
            kh = conv[:, GDN_QK + h * GDN_DK:GDN_QK + (h + 1) * GDN_DK]
            vh = conv[:, 2 * GDN_QK + h * GDN_DV:2 * GDN_QK + (h + 1) * GDN_DV]
            qh = qh * lax.rsqrt(jnp.sum(qh * qh, axis=-1, keepdims=True) + L2_EPS) * (GDN_DK ** -0.5)
            kh = kh * lax.rsqrt(jnp.sum(kh * kh, axis=-1, keepdims=True) + L2_EPS)
            gch = gc[:, hl]
            bth = beta[:, hl]
            gcol = gch[:, 0:c]
            grow = jnp.sum(jnp.where(eye, gcol, 0.0), axis=0, keepdims=True)
            gam = jnp.where(causal, jnp.exp(jnp.minimum(gcol - grow, 0.0)), 0.0)
            s_old = sout_ref[0, h]

            n_mat = jnp.where(strict, bth[:, 0:c] * _mm_nt(kh, kh) * gam, 0.0)
            t_m = -n_mat
            x_pow = n_mat
            for _ in range(n_levels):
                x_pow = _mm(x_pow, x_pow)
                t_m = t_m + x_pow + _mm(t_m, x_pow)

            eg = jnp.exp(gch)
            rhs = bth * (vh - eg * _mm(kh, s_old))
            delta = rhs + _mm(t_m, rhs)
            qk = jnp.where(causal, _mm_nt(qh, kh) * gam, 0.0)
            o_h = _mm(qh * eg, s_old) + _mm(qk, delta)
            g_last = gch[c - 1:c, :]
            ks = kh * jnp.exp(g_last - gch)
            sout_ref[0, h] = s_old * jnp.exp(g_last) + _mm_tn(ks, delta)
            o_ref[0, r0:r0 + c, hl] = _rms_gate(
                o_h, ng_ref[...], hd_ref[0, r0:r0 + c, GDN_CONV_DIM + h * GDN_DV:GDN_CONV_DIM + (h + 1) * GDN_DV])

    @pl.when(t_idx == pl.num_programs(1) - 1)
    def _():
        cbout_ref[0] = xwin_ref[tail:SUBLANES, :]


def _gdn_call(h_gdn, h_small, s0, conv_buf, conv_w, a_log, dt_bias, norm_g, *, chunks_per_step):
    bsz, t_len, _ = h_gdn.shape
    c = min(MAX_CHUNK, t_len)
    tb = c * chunks_per_step
    assert t_len % tb == 0 and c >= SUBLANES and c & (c - 1) == 0
    sel = np.zeros((SMALL_COLS, 2 * GDN_V), np.float32)
    for h in range(GDN_HEADS):
        sel[SMALL_A0 + h, h * LANES:(h + 1) * LANES] = 1.0
        sel[SMALL_B0 + h, GDN_V + h * LANES:GDN_V + (h + 1) * LANES] = 1.0
    sel = jnp.asarray(sel, BF16)
    tri = jnp.asarray(np.tril(np.ones((c, c), np.float32)), BF16)
    alog_b = jnp.repeat(a_log.astype(F32), LANES)[None, :]
    dtb_b = jnp.repeat(dt_bias.astype(F32), LANES)[None, :]
    tok = lambda b, t: (b, t, 0)
    per_b4 = lambda b, t: (b, 0, 0, 0)
    per_b3 = lambda b, t: (b, 0, 0)
    kern = functools.partial(_gdn_kernel, c=c, n_chunks=chunks_per_step)
    return pl.pallas_call(
        kern,
        grid=(bsz, t_len // tb),
        in_specs=[pl.BlockSpec((1, tb, GDN_COLS), tok),
                  pl.BlockSpec((1, tb, SMALL_COLS), tok),
                  pl.BlockSpec((1, GDN_HEADS, GDN_DK, GDN_DV), per_b4),
                  pl.BlockSpec((1, CONV_W - 1, GDN_CONV_DIM), per_b3),
                  _const_spec(conv_w.shape), _const_spec(sel.shape), _const_spec(alog_b.shape),
                  _const_spec(dtb_b.shape), _const_spec(norm_g.shape), _const_spec(tri.shape)],
        out_specs=[pl.BlockSpec((1, tb, GDN_V), tok),
                   pl.BlockSpec((1, GDN_HEADS, GDN_DK, GDN_DV), per_b4),
                   pl.BlockSpec((1, CONV_W - 1, GDN_CONV_DIM), per_b3)],
        out_shape=[jax.ShapeDtypeStruct((bsz, t_len, GDN_V), F32),
                   jax.ShapeDtypeStruct((bsz, GDN_HEADS, GDN_DK, GDN_DV), F32),
                   jax.ShapeDtypeStruct((bsz, CONV_W - 1, GDN_CONV_DIM), F32)],
        scratch_shapes=[pltpu.VMEM((SUBLANES + c, GDN_CONV_DIM), F32)],
        compiler_params=pltpu.CompilerParams(dimension_semantics=("arbitrary", "arbitrary"),
                                             vmem_limit_bytes=VMEM_LIMIT),
        name="gdn",
    )(h_gdn, h_small, s0, conv_buf, conv_w, sel, alog_b, dtb_b, norm_g, tri)


def _out_ffn_kernel(x_ref, og_ref, od_ref, wo_ref, g1_ref, b1_ref, wg_ref, wu_ref, wd_ref,
                    g2_ref, b2_ref, y_ref, hid_ref, *, alpha):
    m = (jnp.dot(og_ref[...].astype(BF16), wo_ref[0:GLA_V, :], preferred_element_type=F32)
         + jnp.dot(od_ref[...].astype(BF16), wo_ref[GLA_V:GLA_V + GDN_V, :], preferred_element_type=F32))
    x1 = _layer_norm(alpha * x_ref[...] + m, g1_ref[...], b1_ref[...])
    x1b = x1.astype(BF16)
    for f0 in range(0, D_FF, FF_TILE):
        gt = jnp.dot(x1b, wg_ref[:, f0:f0 + FF_TILE], preferred_element_type=F32)
        up = jnp.dot(x1b, wu_ref[:, f0:f0 + FF_TILE], preferred_element_type=F32)
        hid_ref[:, f0:f0 + FF_TILE] = (_silu(gt) * up).astype(BF16)
    f = jnp.dot(hid_ref[...], wd_ref[...], preferred_element_type=F32)
    y_ref[...] = _layer_norm(alpha * x1 + f, g2_ref[...], b2_ref[...])


def _out_ffn_call(x2d, o_gla, o_gdn, w_out, ln1_g, ln1_b, w_gate, w_up, w_down, ln2_g, ln2_b, *, alpha):
    n_tok = x2d.shape[0]
    tm = 512 if n_tok % 512 == 0 else n_tok
    row = lambda i: (i, 0)
    kern = functools.partial(_out_ffn_kernel, alpha=alpha)
    return pl.pallas_call(
        kern,
        grid=(n_tok // tm,),
        in_specs=[pl.BlockSpec((tm, D_MODEL), row), pl.BlockSpec((tm, GLA_V), row),
                  pl.BlockSpec((tm, GDN_V), row),
                  _const_spec(w_out.shape), _const_spec(ln1_g.shape), _const_spec(ln1_b.shape),
                  _const_spec(w_gate.shape), _const_spec(w_up.shape), _const_spec(w_down.shape),
                  _const_spec(ln2_g.shape), _const_spec(ln2_b.shape)],
        out_specs=pl.BlockSpec((tm, D_MODEL), row),
        out_shape=jax.ShapeDtypeStruct((n_tok, D_MODEL), F32),
        scratch_shapes=[pltpu.VMEM((tm, D_FF), BF16)],
        compiler_params=pltpu.CompilerParams(dimension_semantics=("arbitrary",),
                                             vmem_limit_bytes=VMEM_LIMIT),
        name="out_ffn",
    )(x2d, o_gla, o_gdn, w_out, ln1_g, ln1_b, w_gate, w_up, w_down, ln2_g, ln2_b)


def _regroup_w_in(w_in):
    o = 0
    q0 = o; o += GLA_QK
    k0 = o; o += GLA_QK
    v0 = o; o += GLA_V
    gg0 = o; o += GLA_V
    ga0 = o; o += GLA_GATE_RANK
    dqkv0 = o; o += GDN_CONV_DIM
    dg0 = o; o += GDN_V
    da0 = o; o += GDN_HEADS
    db0 = o; o += GDN_HEADS
    assert w_in.shape[1] == o and (q0, k0, v0, gg0) == (0, GLA_QK, 2 * GLA_QK, 2 * GLA_QK + GLA_V)
    w_gla = w_in[:, 0:ga0]
    w_gdn = w_in[:, dqkv0:da0]
    pad = jnp.zeros((w_in.shape[0], SMALL_COLS - GLA_GATE_RANK - 2 * GDN_HEADS), w_in.dtype)
    w_small = jnp.concatenate([w_in[:, ga0:dqkv0], w_in[:, da0:o], pad], axis=1)
    return w_gla.astype(BF16), w_gdn.astype(BF16), w_small.astype(BF16)


def _trunk_layer(x, s_gla, s_gdn, conv_buf, p, *, alpha):
    bsz, t_len, _ = x.shape
    x2d = x.reshape(bsz * t_len, D_MODEL)
    h_gla, h_gdn, h_small = _inproj_call(x2d, p["w_gla"], p["w_gdn"], p["w_small"])
    h_gla = h_gla.reshape(bsz, t_len, GLA_COLS)
    h_gdn = h_gdn.reshape(bsz, t_len, GDN_COLS)
    h_small = h_small.reshape(bsz, t_len, SMALL_COLS)
    n_chunks = max(1, t_len // MAX_CHUNK)
    cps = 2 if n_chunks % 2 == 0 else 1
    o_gla, s_gla_new = _gla_call(h_gla, h_small, s_gla, p["wup_pad"], p["b_gate"], p["gla_norm_g"],
                                 chunks_per_step=cps)
    o_gdn, s_gdn_new, buf_new = _gdn_call(h_gdn, h_small, s_gdn, conv_buf, p["conv_w"], p["a_log"],
                                          p["dt_bias"], p["gdn_norm_g"], chunks_per_step=cps)
    y = _out_ffn_call(x2d, o_gla.reshape(bsz * t_len, GLA_V), o_gdn.reshape(bsz * t_len, GDN_V),
                      p["w_out"], p["ln1_g"], p["ln1_b"], p["w_gate"], p["w_up"], p["w_down"],
                      p["ln2_g"], p["ln2_b"], alpha=alpha)
    return y.reshape(bsz, t_len, D_MODEL), s_gla_new, s_gdn_new, buf_new


def kernel(x_prompt, x_sample, state_gla, state_gdn, state_gdn_conv, w_in, gla_w_gate_up, gla_b_gate,
           gla_norm_g, gdn_conv_w, gdn_a_log, gdn_dt_bias, gdn_norm_g, w_out, ln1_g, ln1_b,
           w_ffn_gate, w_ffn_up, w_ffn_down, ln2_g, ln2_b):
    depth = w_in.shape[0]
    alpha = float((2 * depth) ** 0.25)
    bp = x_prompt.shape[0]
    yp, ys = x_prompt, x_sample
    outs = [[] for _ in range(6)]
    for l in range(depth):
        w_gla, w_gdn, w_small = _regroup_w_in(w_in[l])
        wup_pad = jnp.zeros((SMALL_COLS, GLA_QK), F32).at[0:GLA_GATE_RANK].set(
            gla_w_gate_up[l].astype(F32)).astype(BF16)
        p = dict(w_gla=w_gla, w_gdn=w_gdn, w_small=w_small, wup_pad=wup_pad,
                 b_gate=gla_b_gate[l].astype(F32)[None, :], gla_norm_g=gla_norm_g[l].astype(F32)[None, :],
                 conv_w=gdn_conv_w[l].astype(F32), a_log=gdn_a_log[l], dt_bias=gdn_dt_bias[l],
                 gdn_norm_g=gdn_norm_g[l].astype(F32)[None, :], w_out=w_out[l].astype(BF16),
                 ln1_g=ln1_g[l][None, :], ln1_b=ln1_b[l][None, :], w_gate=w_ffn_gate[l].astype(BF16),
                 w_up=w_ffn_up[l].astype(BF16), w_down=w_ffn_down[l].astype(BF16),
                 ln2_g=ln2_g[l][None, :], ln2_b=ln2_b[l][None, :])
        z_gla = jnp.zeros((bp,) + state_gla.shape[2:], state_gla.dtype)
        z_gdn = jnp.zeros((bp,) + state_gdn.shape[2:], state_gdn.dtype)
        z_conv = jnp.zeros((bp,) + state_gdn_conv.shape[2:], state_gdn_conv.dtype)
        yp, a1, a2, a3 = _trunk_layer(yp, z_gla, z_gdn, z_conv, p, alpha=alpha)
        ys, b1, b2, b3 = _trunk_layer(ys, state_gla[l], state_gdn[l], state_gdn_conv[l], p, alpha=alpha)
        for lst, val in zip(outs, (a1, a2, a3, b1, b2, b3)):
            lst.append(val)
    return (yp, ys) + tuple(jnp.stack(o) for o in outs)
```

```python
import functools

import numpy as np
import jax
import jax.numpy as jnp
from jax import lax
from jax.experimental import pallas as pl
from jax.experimental.pallas import tpu as pltpu

F32 = jnp.float32
BF16 = jnp.bfloat16

D_MODEL = 1024
GLA_HEADS, GLA_DK, GLA_DV = 4, 64, 128
GLA_QK = GLA_HEADS * GLA_DK
GLA_V = GLA_HEADS * GLA_DV
GLA_GATE_RANK = 16
GLA_GATE_TAU = 16.0
GDN_HEADS, GDN_DK, GDN_DV = 4, 128, 128
GDN_QK = GDN_HEADS * GDN_DK
GDN_V = GDN_HEADS * GDN_DV
GDN_CONV_DIM = 2 * GDN_QK + GDN_V
CONV_W = 4
D_FF = 2816
NORM_EPS = 1e-5
L2_EPS = 1e-6

GLA_COLS = 2 * GLA_QK + 2 * GLA_V
GDN_COLS = GDN_CONV_DIM + GDN_V
SMALL_COLS = 128
SMALL_A0 = GLA_GATE_RANK
SMALL_B0 = GLA_GATE_RANK + GDN_HEADS

MAX_CHUNK = 64
GLA_SUB = 16
LANES = 128
SUBLANES = 8
VMEM_LIMIT = 56 * 1024 * 1024
FF_TILE = 256


def _mm(a, b):
    return jnp.dot(a.astype(BF16), b.astype(BF16), preferred_element_type=F32)


def _mm_nt(a, b):
    return lax.dot_general(a.astype(BF16), b.astype(BF16), (((1,), (1,)), ((), ())),
                           preferred_element_type=F32)


def _mm_tn(a, b):
    return lax.dot_general(a.astype(BF16), b.astype(BF16), (((0,), (0,)), ((), ())),
                           preferred_element_type=F32)


def _split3(x):
    hi = x.astype(BF16)
    r = x - hi.astype(F32)
    mid = r.astype(BF16)
    lo = (r - mid.astype(F32)).astype(BF16)
    return hi, mid, lo


def _sel_mm(c, x):
    hi, mid, lo = _split3(x)
    d = functools.partial(jnp.dot, preferred_element_type=F32)
    return (d(c, lo) + d(c, mid)) + d(c, hi)


def _mm_sel(x, c):
    hi, mid, lo = _split3(x)
    d = functools.partial(jnp.dot, preferred_element_type=F32)
    return (d(lo, c) + d(mid, c)) + d(hi, c)


def _sigmoid(x):
    return 1.0 / (1.0 + jnp.exp(-x))


def _silu(x):
    return x * _sigmoid(x)


def _softplus(x):
    return jnp.maximum(x, 0.0) + jnp.log1p(jnp.exp(-jnp.abs(x)))


def _log_sigmoid(x):
    return jnp.minimum(x, 0.0) - jnp.log1p(jnp.exp(-jnp.abs(x)))


def _layer_norm(x, g, b):
    mu = jnp.mean(x, axis=-1, keepdims=True)
    xc = x - mu
    var = jnp.mean(xc * xc, axis=-1, keepdims=True)
    return xc * lax.rsqrt(var + NORM_EPS) * g + b


def _rms_gate(o, g, gate):
    ms = jnp.mean(o * o, axis=-1, keepdims=True)
    return o * lax.rsqrt(ms + NORM_EPS) * g * _silu(gate)


def _const_spec(shape):
    nd = len(shape)
    return pl.BlockSpec(shape, lambda *_: (0,) * nd, pipeline_mode=pl.Buffered(1))


def _inproj_kernel(x_ref, wg_ref, wd_ref, ws_ref, og_ref, od_ref, os_ref):
    x = x_ref[...].astype(BF16)
    for w_ref, o_ref in ((wg_ref, og_ref), (wd_ref, od_ref), (ws_ref, os_ref)):
        n = w_ref.shape[1]
        step = min(n, 512)
        for c0 in range(0, n, step):
            o_ref[:, c0:c0 + step] = jnp.dot(x, w_ref[:, c0:c0 + step], preferred_element_type=F32)


def _inproj_call(x2d, w_gla, w_gdn, w_small):
    n_tok = x2d.shape[0]
    tm = 512 if n_tok % 512 == 0 else n_tok
    row = lambda i: (i, 0)
    return pl.pallas_call(
        _inproj_kernel,
        grid=(n_tok // tm,),
        in_specs=[pl.BlockSpec((tm, D_MODEL), row),
                  _const_spec(w_gla.shape), _const_spec(w_gdn.shape), _const_spec(w_small.shape)],
        out_specs=[pl.BlockSpec((tm, GLA_COLS), row), pl.BlockSpec((tm, GDN_COLS), row),
                   pl.BlockSpec((tm, SMALL_COLS), row)],
        out_shape=[jax.ShapeDtypeStruct((n_tok, GLA_COLS), F32),
                   jax.ShapeDtypeStruct((n_tok, GDN_COLS), F32),
                   jax.ShapeDtypeStruct((n_tok, SMALL_COLS), F32)],
        compiler_params=pltpu.CompilerParams(dimension_semantics=("arbitrary",),
                                             vmem_limit_bytes=VMEM_LIMIT),
        name="inproj",
    )(x2d, w_gla, w_gdn, w_small)


def _gla_exponent_matrix(c, sb):
    nb = c // sb
    i = np.arange(c)[:, None]
    t = np.arange(c)[None, :]
    blk = i // sb
    mats = [t <= i, (t > blk * sb) & (t <= i), t > i]
    if nb > 1:
        mats.append((t > i) & (t <= (blk + 1) * sb))
        for m in range(1, nb - 1):
            mats.append((t > (blk - m) * sb) & (t <= blk * sb) & (blk >= m))
    return np.concatenate([m_.astype(np.float32) for m_ in mats], axis=0)


def _gla_kernel(hg_ref, hs_ref, s0_ref, wup_ref, bg_ref, ng_ref, cm_ref, e2_ref,
                o_ref, sout_ref, sbd_ref, eq_ref, *, c, sb, n_chunks):
    nb = c // sb
    t_idx = pl.program_id(1)

    @pl.when(t_idx == 0)
    def _():
        sbd_ref[...] = jnp.zeros_like(sbd_ref)
        for p in range(2):
            sbd_ref[p, 0:GLA_DK, 0:GLA_DV] = s0_ref[0, 2 * p]
            sbd_ref[p, GLA_DK:2 * GLA_DK, GLA_DV:2 * GLA_DV] = s0_ref[0, 2 * p + 1]

    lane = lax.broadcasted_iota(jnp.int32, (1, GLA_QK), 1)
    head_mask = [(lax.shift_right_logical(lane, 6) == h).astype(F32) for h in range(GLA_HEADS)]
    r_bd = lax.shift_right_logical(lax.broadcasted_iota(jnp.int32, (2 * GLA_DK, 2 * GLA_DV), 0), 6)
    c_bd = lax.shift_right_logical(lax.broadcasted_iota(jnp.int32, (2 * GLA_DK, 2 * GLA_DV), 1), 7)
    mask_bd = (r_bd == c_bd).astype(F32)
    eye128 = (lax.broadcasted_iota(jnp.int32, (LANES, LANES), 0)
              == lax.broadcasted_iota(jnp.int32, (LANES, LANES), 1))
    row_in_blk = lax.broadcasted_iota(jnp.int32, (sb, GLA_QK), 0)
    if nb > 1:
        sb_shift = sb.bit_length() - 1
        ri = lax.broadcasted_iota(jnp.int32, (GLA_HEADS * c, c), 0)
        ci = lax.broadcasted_iota(jnp.int32, (GLA_HEADS * c, c), 1)
        blk_diff = (lax.shift_right_logical(jnp.bitwise_and(ri, c - 1), sb_shift)
                    - lax.shift_right_logical(ci, sb_shift))

    for ch in range(n_chunks):
        r0 = ch * c
        hs = hs_ref[0, r0:r0 + c, :]
        z = _mm(hs, wup_ref[...]) + bg_ref[...]
        la = _log_sigmoid(z) * (1.0 / GLA_GATE_TAU)
        ex = _sel_mm(cm_ref[...], la)
        b = ex[0:c]
        eq = ex[c:2 * c]
        es = ex[2 * c:3 * c]
        q = hg_ref[0, r0:r0 + c, 0:GLA_QK] * (GLA_DK ** -0.5)
        k = hg_ref[0, r0:r0 + c, GLA_QK:2 * GLA_QK]
        v = hg_ref[0, r0:r0 + c, 2 * GLA_QK:2 * GLA_QK + GLA_V]
        gate = hg_ref[0, r0:r0 + c, 2 * GLA_QK + GLA_V:GLA_COLS]
        qt = q * jnp.exp(eq)

        o_heads = [None] * GLA_HEADS
        if nb > 1:
            kh = k * jnp.exp(ex[3 * c:4 * c])
            a_off = None
            for d in range(1, nb):
                qd = qt if d == 1 else qt * jnp.exp(ex[(3 + d - 1) * c:(4 + d - 1) * c])
                lhs = jnp.concatenate([qd * head_mask[h] for h in range(GLA_HEADS)], axis=0)
                a_d = jnp.where(blk_diff == d, _mm_nt(lhs, kh), 0.0)
                a_off = a_d if a_off is None else a_off + a_d
            for h in range(GLA_HEADS):
                o_heads[h] = _mm(a_off[h * c:(h + 1) * c], v[:, h * GLA_DV:(h + 1) * GLA_DV])

        eq_ref[...] = eq
        p_rows = []
        for jj in range(sb):
            for bi in range(nb):
                rj = bi * sb + jj
                k_j = hg_ref[0, pl.ds(r0 + rj, 1), GLA_QK:2 * GLA_QK]
                eq_j = eq_ref[pl.ds(rj, 1), :]
                blk = slice(bi * sb, (bi + 1) * sb)
                w = jnp.where(row_in_blk >= jj, jnp.exp(jnp.minimum(eq[blk] - eq_j, 0.0)), 0.0)
                p_rows.append(q[blk] * k_j * w)
        p_all = jnp.concatenate(p_rows, axis=0).astype(BF16)
        coef = jnp.dot(p_all, e2_ref[...], preferred_element_type=F32)
        o_diag = None
        for jj in range(sb):
            v_j = jnp.concatenate(
                [jnp.broadcast_to(hg_ref[0, pl.ds(r0 + bi * sb + jj, 1), 2 * GLA_QK:2 * GLA_QK + GLA_V],
                                  (sb, GLA_V)) for bi in range(nb)], axis=0)
            term = coef[jj * c:(jj + 1) * c] * v_j
            o_diag = term if o_diag is None else o_diag + term

        qhat = q * jnp.exp(b)
        ks = k * jnp.exp(es)
        b_last = b[c - 1:c, :]
        for p in range(2):
            s_old = sbd_ref[p]
            inter = _mm(qhat[:, p * LANES:(p + 1) * LANES], s_old)
            for hh in range(2):
                h = 2 * p + hh
                o_h = inter[:, hh * GLA_DV:(hh + 1) * GLA_DV] + o_diag[:, h * GLA_DV:(h + 1) * GLA_DV]
                if o_heads[h] is not None:
                    o_h = o_h + o_heads[h]
                o_ref[0, r0:r0 + c, h * GLA_DV:(h + 1) * GLA_DV] = _rms_gate(
                    o_h, ng_ref[...], gate[:, h * GLA_DV:(h + 1) * GLA_DV])
            upd = _mm_tn(ks[:, p * LANES:(p + 1) * LANES], v[:, 2 * p * GLA_DV:(2 * p + 2) * GLA_DV])
            bl = jnp.broadcast_to(b_last[:, p * LANES:(p + 1) * LANES], (LANES, LANES))
            decay = jnp.exp(jnp.sum(jnp.where(eye128, bl, 0.0), axis=1, keepdims=True))
            sbd_ref[p] = s_old * decay + upd * mask_bd

    @pl.when(t_idx == pl.num_programs(1) - 1)
    def _():
        for p in range(2):
            sout_ref[0, 2 * p] = sbd_ref[p, 0:GLA_DK, 0:GLA_DV]
            sout_ref[0, 2 * p + 1] = sbd_ref[p, GLA_DK:2 * GLA_DK, GLA_DV:2 * GLA_DV]


def _gla_call(h_gla, h_small, s0, wup_pad, b_gate, norm_g, *, chunks_per_step):
    bsz, t_len, _ = h_gla.shape
    c = min(MAX_CHUNK, t_len)
    sb = min(GLA_SUB, c)
    tb = c * chunks_per_step
    assert t_len % tb == 0 and c % sb == 0
    cm = jnp.asarray(_gla_exponent_matrix(c, sb), BF16)
    e2 = jnp.asarray(np.kron(np.eye(GLA_HEADS, dtype=np.float32),
                             np.ones((GLA_DK, GLA_DV), np.float32)), BF16)
    tok = lambda b, t: (b, t, 0)
    per_b = lambda b, t: (b, 0, 0, 0)
    kern = functools.partial(_gla_kernel, c=c, sb=sb, n_chunks=chunks_per_step)
    return pl.pallas_call(
        kern,
        grid=(bsz, t_len // tb),
        in_specs=[pl.BlockSpec((1, tb, GLA_COLS), tok),
                  pl.BlockSpec((1, tb, SMALL_COLS), tok),
                  pl.BlockSpec((1, GLA_HEADS, GLA_DK, GLA_DV), per_b),
                  _const_spec(wup_pad.shape), _const_spec(b_gate.shape), _const_spec(norm_g.shape),
                  _const_spec(cm.shape), _const_spec(e2.shape)],
        out_specs=[pl.BlockSpec((1, tb, GLA_V), tok),
                   pl.BlockSpec((1, GLA_HEADS, GLA_DK, GLA_DV), per_b)],
        out_shape=[jax.ShapeDtypeStruct((bsz, t_len, GLA_V), F32),
                   jax.ShapeDtypeStruct((bsz, GLA_HEADS, GLA_DK, GLA_DV), F32)],
        scratch_shapes=[pltpu.VMEM((2, 2 * GLA_DK, 2 * GLA_DV), F32),
                        pltpu.VMEM((c, GLA_QK), F32)],
        compiler_params=pltpu.CompilerParams(dimension_semantics=("arbitrary", "arbitrary"),
                                             vmem_limit_bytes=VMEM_LIMIT),
        name="gla",
    )(h_gla, h_small, s0, wup_pad, b_gate, norm_g, cm, e2)


def _gdn_kernel(hd_ref, hs_ref, s0_ref, cb_ref, cw_ref, sel_ref, alog_ref, dtb_ref, ng_ref, tri_ref,
                o_ref, sout_ref, cbout_ref, xwin_ref, *, c, n_seq, n_chunks):
    t_idx = pl.program_id(1)
    tail = SUBLANES - (CONV_W - 1)
    tb = c * n_chunks

    @pl.when(t_idx == 0)
    def _():
        sout_ref[...] = s0_ref[...]
        for s in range(n_seq):
            xwin_ref[s, 0:SUBLANES, :] = jnp.zeros((SUBLANES, GDN_CONV_DIM), F32)
            xwin_ref[s, tail:SUBLANES, :] = cb_ref[s]

    ri = lax.broadcasted_iota(jnp.int32, (c, c), 0)
    ci = lax.broadcasted_iota(jnp.int32, (c, c), 1)
    eye = ri == ci
    causal = ri >= ci
    strict = ri > ci
    n_pow = c.bit_length() - 2
    units = [(s, ch) for s in range(n_seq) for ch in range(n_chunks)]
    heads = range(GDN_HEADS)
    uh = [(u, h) for u in units for h in heads]

    conv = {}
    for s in range(n_seq):
        xwin_ref[s, SUBLANES:SUBLANES + c, :] = hd_ref[s, 0:c, 0:GDN_CONV_DIM]
        for ch in range(n_chunks):
            r0 = ch * c
            acc = None
            for i in range(CONV_W):
                if ch == 0:
                    rows = xwin_ref[s, tail + i:tail + i + c, :]
                else:
                    rows = hd_ref[s, r0 - (CONV_W - 1) + i:r0 - (CONV_W - 1) + i + c, 0:GDN_CONV_DIM]
                term = rows * cw_ref[i:i + 1, :]
                acc = term if acc is None else acc + term
            conv[(s, ch)] = _silu(acc)
        xwin_ref[s, 0:SUBLANES, :] = hd_ref[s, tb - SUBLANES:tb, 0:GDN_CONV_DIM]

    ab = {u: _mm_sel(hs_ref[u[0], u[1] * c:(u[1] + 1) * c, :], sel_ref[...]) for u in units}
    g = {u: -jnp.exp(alog_ref[...]) * _softplus(ab[u][:, 0:GDN_V] + dtb_ref[...]) for u in units}
    beta = {u: _sigmoid(ab[u][:, GDN_V:2 * GDN_V]) for u in units}
    gc = {u: _sel_mm(tri_ref[...], g[u]) for u in units}

    q, k, gam, eg, bt = {}, {}, {}, {}, {}
    for (u, h) in uh:
        cv = conv[u]
        qh = cv[:, h * GDN_DK:(h + 1) * GDN_DK]
        kh = cv[:, GDN_QK + h * GDN_DK:GDN_QK + (h + 1) * GDN_DK]
        q[u, h] = qh * lax.rsqrt(jnp.sum(qh * qh, axis=-1, keepdims=True) + L2_EPS) * (GDN_DK ** -0.5)
        k[u, h] = kh * lax.rsqrt(jnp.sum(kh * kh, axis=-1, keepdims=True) + L2_EPS)
        gch = gc[u][:, h * LANES:(h + 1) * LANES]
        gcol = gch[:, 0:c]
        grow = jnp.sum(jnp.where(eye, gcol, 0.0), axis=0, keepdims=True)
        gam[u, h] = jnp.where(causal, jnp.exp(jnp.minimum(gcol - grow, 0.0)), 0.0)
        eg[u, h] = jnp.exp(gch)
        bt[u, h] = beta[u][:, h * LANES:(h + 1) * LANES]

    qkk = {x: _mm_nt(jnp.concatenate([q[x], k[x]], axis=0), k[x]) for x in uh}
    qk = {x: jnp.where(causal, qkk[x][0:c] * gam[x], 0.0) for x in uh}
    n_mat = {x: jnp.where(strict, bt[x][:, 0:c] * qkk[x][c:2 * c] * gam[x], 0.0) for x in uh}

    t_m = {x: -n_mat[x] for x in uh}
    x_pow = {x: _mm(n_mat[x], n_mat[x]) for x in uh}
    for _ in range(n_pow - 1):
        both = {x: _mm(jnp.concatenate([x_pow[x], t_m[x]], axis=0), x_pow[x]) for x in uh}
        t_m = {x: t_m[x] + x_pow[x] + both[x][c:2 * c] for x in uh}
        x_pow = {x: both[x][0:c] for x in uh}
    last = {x: _mm(t_m[x], x_pow[x]) for x in uh}
    t_m = {x: t_m[x] + x_pow[x] + last[x] for x in uh}

    rhs = {}
    for (u, h) in uh:
        vh = conv[u][:, 2 * GDN_QK + h * GDN_DV:2 * GDN_QK + (h + 1) * GDN_DV]
        rhs[u, h] = jnp.concatenate([bt[u, h] * vh, bt[u, h] * eg[u, h] * k[u, h]], axis=1)
    uw = {x: rhs[x] + _mm(t_m[x], rhs[x]) for x in uh}

    for ch in range(n_chunks):
        cur = [((s, ch), h) for s in range(n_seq) for h in heads]
        s_old = {x: sout_ref[x[0][0], x[1]] for x in cur}
        ws_qs = {x: _mm(jnp.concatenate([uw[x][:, GDN_DV:2 * GDN_DV], q[x] * eg[x]], axis=0), s_old[x])
                 for x in cur}
        delta = {x: uw[x][:, 0:GDN_DV] - ws_qs[x][0:c] for x in cur}
        for x in cur:
            (s, _), h = x
            hl = slice(h * LANES, (h + 1) * LANES)
            gch = gc[x[0]][:, hl]
            g_last = gch[c - 1:c, :]
            ks = k[x] * jnp.exp(g_last - gch)
            sout_ref[s, h] = s_old[x] * jnp.exp(g_last) + _mm_tn(ks, delta[x])
            o_h = ws_qs[x][c:2 * c] + _mm(qk[x], delta[x])
            r0 = ch * c
            o_ref[s, r0:r0 + c, hl] = _rms_gate(
                o_h, ng_ref[...], hd_ref[s, r0:r0 + c, GDN_CONV_DIM + h * GDN_DV:GDN_CONV_DIM + (h + 1) * GDN_DV])

    @pl.when(t_idx == pl.num_programs(1) - 1)
    def _():
        for s in range(n_seq):
            cbout_ref[s] = xwin_ref[s, tail:SUBLANES, :]


def _gdn_call(h_gdn, h_small, s0, conv_buf, conv_w, a_log, dt_bias, norm_g, *, seqs_per_step, chunks_per_step):
    bsz, t_len, _ = h_gdn.shape
    c = min(MAX_CHUNK, t_len)
    tb = c * chunks_per_step
    bb = seqs_per_step
    assert t_len % tb == 0 and bsz % bb == 0 and c >= SUBLANES and c & (c - 1) == 0
    sel = np.zeros((SMALL_COLS, 2 * GDN_V), np.float32)
    for h in range(GDN_HEADS):
        sel[SMALL_A0 + h, h * LANES:(h + 1) * LANES] = 1.0
        sel[SMALL_B0 + h, GDN_V + h * LANES:GDN_V + (h + 1) * LANES] = 1.0
    sel = jnp.asarray(sel, BF16)
    tri = jnp.asarray(np.tril(np.ones((c, c), np.float32)), BF16)
    alog_b = jnp.repeat(a_log.astype(F32), LANES)[None, :]
    dtb_b = jnp.repeat(dt_bias.astype(F32), LANES)[None, :]
    tok = lambda b, t: (b, t, 0)
    per_b4 = lambda b, t: (b, 0, 0, 0)
    per_b3 = lambda b, t: (b, 0, 0)
    kern = functools.partial(_gdn_kernel, c=c, n_seq=bb, n_chunks=chunks_per_step)
    return pl.pallas_call(
        kern,
        grid=(bsz // bb, t_len // tb),
        in_specs=[pl.BlockSpec((bb, tb, GDN_COLS), tok),
                  pl.BlockSpec((bb, tb, SMALL_COLS), tok),
                  pl.BlockSpec((bb, GDN_HEADS, GDN_DK, GDN_DV), per_b4),
                  pl.BlockSpec((bb, CONV_W - 1, GDN_CONV_DIM), per_b3),
                  _const_spec(conv_w.shape), _const_spec(sel.shape), _const_spec(alog_b.shape),
                  _const_spec(dtb_b.shape), _const_spec(norm_g.shape), _const_spec(tri.shape)],
        out_specs=[pl.BlockSpec((bb, tb, GDN_V), tok),
                   pl.BlockSpec((bb, GDN_HEADS, GDN_DK, GDN_DV), per_b4),
                   pl.BlockSpec((bb, CONV_W - 1, GDN_CONV_DIM), per_b3)],
        out_shape=[jax.ShapeDtypeStruct((bsz, t_len, GDN_V), F32),
                   jax.ShapeDtypeStruct((bsz, GDN_HEADS, GDN_DK, GDN_DV), F32),
                   jax.ShapeDtypeStruct((bsz, CONV_W - 1, GDN_CONV_DIM), F32)],
        scratch_shapes=[pltpu.VMEM((bb, SUBLANES + c, GDN_CONV_DIM), F32)],
        compiler_params=pltpu.CompilerParams(dimension_semantics=("arbitrary", "arbitrary"),
                                             vmem_limit_bytes=VMEM_LIMIT),
        name="gdn",
    )(h_gdn, h_small, s0, conv_buf, conv_w, sel, alog_b, dtb_b, norm_g, tri)


def _out_ffn_kernel(x_ref, og_ref, od_ref, wo_ref, g1_ref, b1_ref, wg_ref, wu_ref, wd_ref,
                    g2_ref, b2_ref, y_ref, hid_ref, *, alpha):
    m = (jnp.dot(og_ref[...].astype(BF16), wo_ref[0:GLA_V, :], preferred_element_type=F32)
         + jnp.dot(od_ref[...].astype(BF16), wo_ref[GLA_V:GLA_V + GDN_V, :], preferred_element_type=F32))
    x1 = _layer_norm(alpha * x_ref[...] + m, g1_ref[...], b1_ref[...])
    x1b = x1.astype(BF16)
    for f0 in range(0, D_FF, FF_TILE):
        gt = jnp.dot(x1b, wg_ref[:, f0:f0 + FF_TILE], preferred_element_type=F32)
        up = jnp.dot(x1b, wu_ref[:, f0:f0 + FF_TILE], preferred_element_type=F32)
        hid_ref[:, f0:f0 + FF_TILE] = (_silu(gt) * up).astype(BF16)
    f = jnp.dot(hid_ref[...], wd_ref[...], preferred_element_type=F32)
    y_ref[...] = _layer_norm(alpha * x1 + f, g2_ref[...], b2_ref[...])


def _out_ffn_call(x2d, o_gla, o_gdn, w_out, ln1_g, ln1_b, w_gate, w_up, w_down, ln2_g, ln2_b, *, alpha):
    n_tok = x2d.shape[0]
    tm = 512 if n_tok % 512 == 0 else n_tok
    row = lambda i: (i, 0)
    kern = functools.partial(_out_ffn_kernel, alpha=alpha)
    return pl.pallas_call(
        kern,
        grid=(n_tok // tm,),
        in_specs=[pl.BlockSpec((tm, D_MODEL), row), pl.BlockSpec((tm, GLA_V), row),
                  pl.BlockSpec((tm, GDN_V), row),
                  _const_spec(w_out.shape), _const_spec(ln1_g.shape), _const_spec(ln1_b.shape),
                  _const_spec(w_gate.shape), _const_spec(w_up.shape), _const_spec(w_down.shape),
                  _const_spec(ln2_g.shape), _const_spec(ln2_b.shape)],
        out_specs=pl.BlockSpec((tm, D_MODEL), row),
        out_shape=jax.ShapeDtypeStruct((n_tok, D_MODEL), F32),
        scratch_shapes=[pltpu.VMEM((tm, D_FF), BF16)],
        compiler_params=pltpu.CompilerParams(dimension_semantics=("arbitrary",),
                                             vmem_limit_bytes=VMEM_LIMIT),
        name="out_ffn",
    )(x2d, o_gla, o_gdn, w_out, ln1_g, ln1_b, w_gate, w_up, w_down, ln2_g, ln2_b)


def _regroup_w_in(w_in):
    o = 0
    q0 = o; o += GLA_QK
    k0 = o; o += GLA_QK
    v0 = o; o += GLA_V
    gg0 = o; o += GLA_V
    ga0 = o; o += GLA_GATE_RANK
    dqkv0 = o; o += GDN_CONV_DIM
    dg0 = o; o += GDN_V
    da0 = o; o += GDN_HEADS
    db0 = o; o += GDN_HEADS
    assert w_in.shape[1] == o and (q0, k0, v0, gg0) == (0, GLA_QK, 2 * GLA_QK, 2 * GLA_QK + GLA_V)
    w_gla = w_in[:, 0:ga0]
    w_gdn = w_in[:, dqkv0:da0]
    pad = jnp.zeros((w_in.shape[0], SMALL_COLS - GLA_GATE_RANK - 2 * GDN_HEADS), w_in.dtype)
    w_small = jnp.concatenate([w_in[:, ga0:dqkv0], w_in[:, da0:o], pad], axis=1)
    return w_gla.astype(BF16), w_gdn.astype(BF16), w_small.astype(BF16)


def _trunk_layer(x, s_gla, s_gdn, conv_buf, p, *, alpha):
    bsz, t_len, _ = x.shape
    x2d = x.reshape(bsz * t_len, D_MODEL)
    h_gla, h_gdn, h_small = _inproj_call(x2d, p["w_gla"], p["w_gdn"], p["w_small"])
    h_gla = h_gla.reshape(bsz, t_len, GLA_COLS)
    h_gdn = h_gdn.reshape(bsz, t_len, GDN_COLS)
    h_small = h_small.reshape(bsz, t_len, SMALL_COLS)
    n_chunks = max(1, t_len // MAX_CHUNK)
    cps = 2 if n_chunks % 2 == 0 else 1
    o_gla, s_gla_new = _gla_call(h_gla, h_small, s_gla, p["wup_pad"], p["b_gate"], p["gla_norm_g"],
                                 chunks_per_step=cps)
    gdn_cps = 4 if n_chunks % 4 == 0 else 1
    gdn_sps = 1 if gdn_cps > 1 else (4 if bsz % 4 == 0 else 1)
    o_gdn, s_gdn_new, buf_new = _gdn_call(h_gdn, h_small, s_gdn, conv_buf, p["conv_w"], p["a_log"],
                                          p["dt_bias"], p["gdn_norm_g"], seqs_per_step=gdn_sps,
                                          chunks_per_step=gdn_cps)
    y = _out_ffn_call(x2d, o_gla.reshape(bsz * t_len, GLA_V), o_gdn.reshape(bsz * t_len, GDN_V),
                      p["w_out"], p["ln1_g"], p["ln1_b"], p["w_gate"], p["w_up"], p["w_down"],
                      p["ln2_g"], p["ln2_b"], alpha=alpha)
    return y.reshape(bsz, t_len, D_MODEL), s_gla_new, s_gdn_new, buf_new


def kernel(x_prompt, x_sample, state_gla, state_gdn, state_gdn_conv, w_in, gla_w_gate_up, gla_b_gate,
           gla_norm_g, gdn_conv_w, gdn_a_log, gdn_dt_bias, gdn_norm_g, w_out, ln1_g, ln1_b,
           w_ffn_gate, w_ffn_up, w_ffn_down, ln2_g, ln2_b):
    depth = w_in.shape[0]
    alpha = float((2 * depth) ** 0.25)
    bp = x_prompt.shape[0]
    yp, ys = x_prompt, x_sample
    outs = [[] for _ in range(6)]
    for l in range(depth):
        w_gla, w_gdn, w_small = _regroup_w_in(w_in[l])
        wup_pad = jnp.zeros((SMALL_COLS, GLA_QK), F32).at[0:GLA_GATE_RANK].set(
            gla_w_gate_up[l].astype(F32)).astype(BF16)
        p = dict(w_gla=w_gla, w_gdn=w_gdn, w_small=w_small, wup_pad=wup_pad,
                 b_gate=gla_b_gate[l].astype(F32)[None, :], gla_norm_g=gla_norm_g[l].astype(F32)[None, :],
                 conv_w=gdn_conv_w[l].astype(F32), a_log=gdn_a_log[l], dt_bias=gdn_dt_bias[l],
                 gdn_norm_g=gdn_norm_g[l].astype(F32)[None, :], w_out=w_out[l].astype(BF16),
                 ln1_g=ln1_g[l][None, :], ln1_b=ln1_b[l][None, :], w_gate=w_ffn_gate[l].astype(BF16),
                 w_up=w_ffn_up[l].astype(BF16), w_down=w_ffn_down[l].astype(BF16),
                 ln2_g=ln2_g[l][None, :], ln2_b=ln2_b[l][None, :])
        z_gla = jnp.zeros((bp,) + state_gla.shape[2:], state_gla.dtype)
        z_gdn = jnp.zeros((bp,) + state_gdn.shape[2:], state_gdn.dtype)
        z_conv = jnp.zeros((bp,) + state_gdn_conv.shape[2:], state_gdn_conv.dtype)
        yp, a1, a2, a3 = _trunk_layer(yp, z_gla, z_gdn, z_conv, p, alpha=alpha)
        ys, b1, b2, b3 = _trunk_layer(ys, state_gla[l], state_gdn[l], state_gdn_conv[l], p, alpha=alpha)
        for lst, val in zip(outs, (a1, a2, a3, b1, b2, b3)):
            lst.append(val)
    return (yp, ys) + tuple(jnp.stack(o) for o in outs)
```

```python
import functools

import numpy as np
import jax
import jax.numpy as jnp
from jax import lax
from jax.experimental import pallas as pl
from jax.experimental.pallas import tpu as pltpu

F32 = jnp.float32
BF16 = jnp.bfloat16

D_MODEL = 1024
GLA_HEADS, GLA_DK, GLA_DV = 4, 64, 128
GLA_QK = GLA_HEADS * GLA_DK
GLA_V = GLA_HEADS * GLA_DV
GLA_GATE_RANK = 16
GLA_GATE_TAU = 16.0
GDN_HEADS, GDN_DK, GDN_DV = 4, 128, 128
GDN_QK = GDN_HEADS * GDN_DK
GDN_V = GDN_HEADS * GDN_DV
GDN_CONV_DIM = 2 * GDN_QK + GDN_V
CONV_W = 4
D_FF = 2816
NORM_EPS = 1e-5
L2_EPS = 1e-6

GLA_COLS = 2 * GLA_QK + 2 * GLA_V
GDN_COLS = GDN_CONV_DIM + GDN_V
SMALL_COLS = 128
SMALL_A0 = GLA_GATE_RANK
SMALL_B0 = GLA_GATE_RANK + GDN_HEADS

MAX_CHUNK = 64
LOG2_E = 1.4426950408889634
LANES = 128
SUBLANES = 8
VMEM_LIMIT = 56 * 1024 * 1024
FF_TILE = 256


def _mm(a, b):
    return jnp.dot(a.astype(BF16), b.astype(BF16), preferred_element_type=F32)


def _mm_nt(a, b):
    return lax.dot_general(a.astype(BF16), b.astype(BF16), (((1,), (1,)), ((), ())),
                           preferred_element_type=F32)


def _mm_tn(a, b):
    return lax.dot_general(a.astype(BF16), b.astype(BF16), (((0,), (0,)), ((), ())),
                           preferred_element_type=F32)


def _split3(x):
    hi = x.astype(BF16)
    r = x - hi.astype(F32)
    mid = r.astype(BF16)
    lo = (r - mid.astype(F32)).astype(BF16)
    return hi, mid, lo


def _sel_mm(c, x):
    hi, mid, lo = _split3(x)
    d = functools.partial(jnp.dot, preferred_element_type=F32)
    return (d(c, lo) + d(c, mid)) + d(c, hi)


def _mm_sel(x, c):
    hi, mid, lo = _split3(x)
    d = functools.partial(jnp.dot, preferred_element_type=F32)
    return (d(lo, c) + d(mid, c)) + d(hi, c)


def _sigmoid(x):
    return 1.0 / (1.0 + jnp.exp(-x))


def _silu(x):
    return x * _sigmoid(x)


def _softplus(x):
    return jnp.maximum(x, 0.0) + jnp.log1p(jnp.exp(-jnp.abs(x)))


def _log_sigmoid(x):
    return jnp.minimum(x, 0.0) - jnp.log1p(jnp.exp(-jnp.abs(x)))


def _layer_norm(x, g, b):
    mu = jnp.mean(x, axis=-1, keepdims=True)
    xc = x - mu
    var = jnp.mean(xc * xc, axis=-1, keepdims=True)
    return xc * lax.rsqrt(var + NORM_EPS) * g + b


def _rms_gate(o, g, gate):
    ms = jnp.mean(o * o, axis=-1, keepdims=True)
    return o * lax.rsqrt(ms + NORM_EPS) * g * _silu(gate)


def _const_spec(shape):
    nd = len(shape)
    return pl.BlockSpec(shape, lambda *_: (0,) * nd, pipeline_mode=pl.Buffered(1))


def _inproj_kernel(x_ref, wg_ref, wd_ref, ws_ref, og_ref, od_ref, os_ref):
    x = x_ref[...].astype(BF16)
    for w_ref, o_ref in ((wg_ref, og_ref), (wd_ref, od_ref), (ws_ref, os_ref)):
        n = w_ref.shape[1]
        step = min(n, 512)
        for c0 in range(0, n, step):
            o_ref[:, c0:c0 + step] = jnp.dot(x, w_ref[:, c0:c0 + step], preferred_element_type=F32)


def _inproj_call(x2d, w_gla, w_gdn, w_small):
    n_tok = x2d.shape[0]
    tm = 512 if n_tok % 512 == 0 else n_tok
    row = lambda i: (i, 0)
    return pl.pallas_call(
        _inproj_kernel,
        grid=(n_tok // tm,),
        in_specs=[pl.BlockSpec((tm, D_MODEL), row),
                  _const_spec(w_gla.shape), _const_spec(w_gdn.shape), _const_spec(w_small.shape)],
        out_specs=[pl.BlockSpec((tm, GLA_COLS), row), pl.BlockSpec((tm, GDN_COLS), row),
                   pl.BlockSpec((tm, SMALL_COLS), row)],
        out_shape=[jax.ShapeDtypeStruct((n_tok, GLA_COLS), F32),
                   jax.ShapeDtypeStruct((n_tok, GDN_COLS), F32),
                   jax.ShapeDtypeStruct((n_tok, SMALL_COLS), F32)],
        compiler_params=pltpu.CompilerParams(dimension_semantics=("arbitrary",),
                                             vmem_limit_bytes=VMEM_LIMIT),
        name="inproj",
    )(x2d, w_gla, w_gdn, w_small)


def _gla_exponent_matrix(c):
    i = np.arange(c)[:, None]
    t = np.arange(c)[None, :]
    mats = [t <= i, t > i]
    size = c
    while size >= 2:
        m = (i // size) * size + size // 2
        mats.append((t > m) & (t <= i))
        mats.append((t > i) & (t <= m))
        size //= 2
    return np.concatenate([m_.astype(np.float32) for m_ in mats], axis=0)


def _gla_kernel(hg_ref, hs_ref, s0_ref, wup_ref, bg_ref, ng_ref, cm_ref,
                o_ref, sout_ref, sbd_ref, *, c, n_seq, n_chunks):
    t_idx = pl.program_id(1)
    n_lv = c.bit_length() - 1
    pair = lambda p: slice(p * LANES, (p + 1) * LANES)

    @pl.when(t_idx == 0)
    def _():
        sbd_ref[...] = jnp.zeros_like(sbd_ref)
        for s in range(n_seq):
            for p in range(2):
                sbd_ref[s, p, 0:GLA_DK, 0:GLA_DV] = s0_ref[s, 2 * p]
                sbd_ref[s, p, GLA_DK:2 * GLA_DK, GLA_DV:2 * GLA_DV] = s0_ref[s, 2 * p + 1]

    lane = lax.broadcasted_iota(jnp.int32, (1, GLA_QK), 1)
    head_mask = [(lax.shift_right_logical(lane, 6) == h).astype(F32) for h in range(GLA_HEADS)]
    r_bd = lax.shift_right_logical(lax.broadcasted_iota(jnp.int32, (2 * GLA_DK, 2 * GLA_DV), 0), 6)
    c_bd = lax.shift_right_logical(lax.broadcasted_iota(jnp.int32, (2 * GLA_DK, 2 * GLA_DV), 1), 7)
    mask_bd = (r_bd == c_bd).astype(F32)
    eye128 = (lax.broadcasted_iota(jnp.int32, (LANES, LANES), 0)
              == lax.broadcasted_iota(jnp.int32, (LANES, LANES), 1))
    ri = jnp.bitwise_and(lax.broadcasted_iota(jnp.int32, (GLA_HEADS * c, c), 0), c - 1)
    ci = lax.broadcasted_iota(jnp.int32, (GLA_HEADS * c, c), 1)
    pair_level = jnp.where(ri >= ci, lax.clz(jnp.bitwise_xor(ri, ci)), -1)

    units = [(s, ch) for s in range(n_seq) for ch in range(n_chunks)]
    rows = {u: slice(u[1] * c, (u[1] + 1) * c) for u in units}

    z = {u: _mm(hs_ref[u[0], rows[u], :], wup_ref[...]) + bg_ref[...] for u in units}
    la = {u: _log_sigmoid(z[u]) * (LOG2_E / GLA_GATE_TAU) for u in units}
    ex = {u: _sel_mm(cm_ref[...], la[u]) for u in units}
    q = {u: hg_ref[u[0], rows[u], 0:GLA_QK] * (GLA_DK ** -0.5) for u in units}
    k = {u: hg_ref[u[0], rows[u], GLA_QK:2 * GLA_QK] for u in units}
    qm = {u: [q[u] * head_mask[h] for h in range(GLA_HEADS)] for u in units}

    score = {u: jnp.zeros((GLA_HEADS * c, c), F32) for u in units}
    for lv in range(n_lv):
        code = 31 - (n_lv - lv - 1)
        for u in units:
            e_q = ex[u][(2 + 2 * lv) * c:(3 + 2 * lv) * c]
            e_k = ex[u][(3 + 2 * lv) * c:(4 + 2 * lv) * c]
            w_q = jnp.exp2(e_q)
            lhs = jnp.concatenate([qm[u][h] * w_q for h in range(GLA_HEADS)], axis=0)
            a = _mm_nt(lhs, k[u] * jnp.exp2(e_k))
            score[u] = jnp.where(pair_level == code, a, score[u])
    for u in units:
        a = _mm_nt(jnp.concatenate(qm[u], axis=0), k[u])
        score[u] = jnp.where(pair_level == 32, a, score[u])

    v = {u: hg_ref[u[0], rows[u], 2 * GLA_QK:2 * GLA_QK + GLA_V] for u in units}
    o_intra = {(u, h): _mm(score[u][h * c:(h + 1) * c], v[u][:, h * GLA_DV:(h + 1) * GLA_DV])
               for u in units for h in range(GLA_HEADS)}

    qhat = {u: q[u] * jnp.exp2(ex[u][0:c]) for u in units}
    upd, decay = {}, {}
    for u in units:
        ks = k[u] * jnp.exp2(ex[u][c:2 * c])
        b_last = ex[u][c - 1:c, :]
        for p in range(2):
            upd[u, p] = _mm_tn(ks[:, pair(p)], v[u][:, 2 * p * GLA_DV:(2 * p + 2) * GLA_DV]) * mask_bd
            bl = jnp.broadcast_to(b_last[:, pair(p)], (LANES, LANES))
            decay[u, p] = jnp.exp2(jnp.sum(jnp.where(eye128, bl, 0.0), axis=1, keepdims=True))

    for ch in range(n_chunks):
        for s in range(n_seq):
            u = (s, ch)
            for p in range(2):
                s_old = sbd_ref[s, p]
                inter = _mm(qhat[u][:, pair(p)], s_old)
                sbd_ref[s, p] = s_old * decay[u, p] + upd[u, p]
                for hh in range(2):
                    h = 2 * p + hh
                    hv = slice(h * GLA_DV, (h + 1) * GLA_DV)
                    o_h = inter[:, hh * GLA_DV:(hh + 1) * GLA_DV] + o_intra[u, h]
                    gate = hg_ref[s, rows[u], 2 * GLA_QK + GLA_V + h * GLA_DV:2 * GLA_QK + GLA_V + (h + 1) * GLA_DV]
                    o_ref[s, rows[u], hv] = _rms_gate(o_h, ng_ref[...], gate)

    @pl.when(t_idx == pl.num_programs(1) - 1)
    def _():
        for s in range(n_seq):
            for p in range(2):
                sout_ref[s, 2 * p] = sbd_ref[s, p, 0:GLA_DK, 0:GLA_DV]
                sout_ref[s, 2 * p + 1] = sbd_ref[s, p, GLA_DK:2 * GLA_DK, GLA_DV:2 * GLA_DV]


def _gla_call(h_gla, h_small, s0, wup_pad, b_gate, norm_g, *, seqs_per_step, chunks_per_step):
    bsz, t_len, _ = h_gla.shape
    c = min(MAX_CHUNK, t_len)
    tb = c * chunks_per_step
    bb = seqs_per_step
    assert t_len % tb == 0 and bsz % bb == 0 and c >= SUBLANES and c & (c - 1) == 0
    cm = jnp.asarray(_gla_exponent_matrix(c), BF16)
    tok = lambda b, t: (b, t, 0)
    per_b = lambda b, t: (b, 0, 0, 0)
    kern = functools.partial(_gla_kernel, c=c, n_seq=bb, n_chunks=chunks_per_step)
    return pl.pallas_call(
        kern,
        grid=(bsz // bb, t_len // tb),
        in_specs=[pl.BlockSpec((bb, tb, GLA_COLS), tok),
                  pl.BlockSpec((bb, tb, SMALL_COLS), tok),
                  pl.BlockSpec((bb, GLA_HEADS, GLA_DK, GLA_DV), per_b),
                  _const_spec(wup_pad.shape), _const_spec(b_gate.shape), _const_spec(norm_g.shape),
                  _const_spec(cm.shape)],
        out_specs=[pl.BlockSpec((bb, tb, GLA_V), tok),
                   pl.BlockSpec((bb, GLA_HEADS, GLA_DK, GLA_DV), per_b)],
        out_shape=[jax.ShapeDtypeStruct((bsz, t_len, GLA_V), F32),
                   jax.ShapeDtypeStruct((bsz, GLA_HEADS, GLA_DK, GLA_DV), F32)],
        scratch_shapes=[pltpu.VMEM((bb, 2, 2 * GLA_DK, 2 * GLA_DV), F32)],
        compiler_params=pltpu.CompilerParams(dimension_semantics=("arbitrary", "arbitrary"),
                                             vmem_limit_bytes=VMEM_LIMIT),
        name="gla",
    )(h_gla, h_small, s0, wup_pad, b_gate, norm_g, cm)


def _gdn_kernel(hd_ref, hs_ref, s0_ref, cb_ref, cw_ref, sel_ref, alog_ref, dtb_ref, ng_ref, tri_ref,
                o_ref, sout_ref, cbout_ref, xwin_ref, *, c, n_seq, n_chunks):
    t_idx = pl.program_id(1)
    tail = SUBLANES - (CONV_W - 1)
    tb = c * n_chunks

    @pl.when(t_idx == 0)
    def _():
        sout_ref[...] = s0_ref[...]
        for s in range(n_seq):
            xwin_ref[s, 0:SUBLANES, :] = jnp.zeros((SUBLANES, GDN_CONV_DIM), F32)
            xwin_ref[s, tail:SUBLANES, :] = cb_ref[s]

    ri = lax.broadcasted_iota(jnp.int32, (c, c), 0)
    ci = lax.broadcasted_iota(jnp.int32, (c, c), 1)
    eye = ri == ci
    causal = ri >= ci
    strict = ri > ci
    n_pow = c.bit_length() - 2
    units = [(s, ch) for s in range(n_seq) for ch in range(n_chunks)]
    heads = range(GDN_HEADS)
    uh = [(u, h) for u in units for h in heads]

    conv = {}
    for s in range(n_seq):
        xwin_ref[s, SUBLANES:SUBLANES + c, :] = hd_ref[s, 0:c, 0:GDN_CONV_DIM]
        for ch in range(n_chunks):
            r0 = ch * c
            acc = None
            for i in range(CONV_W):
                if ch == 0:
                    rows = xwin_ref[s, tail + i:tail + i + c, :]
                else:
                    rows = hd_ref[s, r0 - (CONV_W - 1) + i:r0 - (CONV_W - 1) + i + c, 0:GDN_CONV_DIM]
                term = rows * cw_ref[i:i + 1, :]
                acc = term if acc is None else acc + term
            conv[(s, ch)] = _silu(acc)
        xwin_ref[s, 0:SUBLANES, :] = hd_ref[s, tb - SUBLANES:tb, 0:GDN_CONV_DIM]

    ab = {u: _mm_sel(hs_ref[u[0], u[1] * c:(u[1] + 1) * c, :], sel_ref[...]) for u in units}
    g = {u: -jnp.exp(alog_ref[...]) * _softplus(ab[u][:, 0:GDN_V] + dtb_ref[...]) for u in units}
    beta = {u: _sigmoid(ab[u][:, GDN_V:2 * GDN_V]) for u in units}
    gc = {u: _sel_mm(tri_ref[...], g[u]) for u in units}

    q, k, gam, eg, bt = {}, {}, {}, {}, {}
    for (u, h) in uh:
        cv = conv[u]
        qh = cv[:, h * GDN_DK:(h + 1) * GDN_DK]
        kh = cv[:, GDN_QK + h * GDN_DK:GDN_QK + (h + 1) * GDN_DK]
        q[u, h] = qh * lax.rsqrt(jnp.sum(qh * qh, axis=-1, keepdims=True) + L2_EPS) * (GDN_DK ** -0.5)
        k[u, h] = kh * lax.rsqrt(jnp.sum(kh * kh, axis=-1, keepdims=True) + L2_EPS)
        gch = gc[u][:, h * LANES:(h + 1) * LANES]
        gcol = gch[:, 0:c]
        grow = jnp.sum(jnp.where(eye, gcol, 0.0), axis=0, keepdims=True)
        gam[u, h] = jnp.where(causal, jnp.exp(jnp.minimum(gcol - grow, 0.0)), 0.0)
        eg[u, h] = jnp.exp(gch)
        bt[u, h] = beta[u][:, h * LANES:(h + 1) * LANES]

    qkk = {x: _mm_nt(jnp.concatenate([q[x], k[x]], axis=0), k[x]) for x in uh}
    qk = {x: jnp.where(causal, qkk[x][0:c] * gam[x], 0.0) for x in uh}
    n_mat = {x: jnp.where(strict, bt[x][:, 0:c] * qkk[x][c:2 * c] * gam[x], 0.0) for x in uh}

    t_m = {x: -n_mat[x] for x in uh}
    x_pow = {x: _mm(n_mat[x], n_mat[x]) for x in uh}
    for _ in range(n_pow - 1):
        both = {x: _mm(jnp.concatenate([x_pow[x], t_m[x]], axis=0), x_pow[x]) for x in uh}
        t_m = {x: t_m[x] + x_pow[x] + both[x][c:2 * c] for x in uh}
        x_pow = {x: both[x][0:c] for x in uh}
    last = {x: _mm(t_m[x], x_pow[x]) for x in uh}
    t_m = {x: t_m[x] + x_pow[x] + last[x] for x in uh}

    rhs = {}
    for (u, h) in uh:
        vh = conv[u][:, 2 * GDN_QK + h * GDN_DV:2 * GDN_QK + (h + 1) * GDN_DV]
        rhs[u, h] = jnp.concatenate([bt[u, h] * vh, bt[u, h] * eg[u, h] * k[u, h]], axis=1)
    uw = {x: rhs[x] + _mm(t_m[x], rhs[x]) for x in uh}

    for ch in range(n_chunks):
        cur = [((s, ch), h) for s in range(n_seq) for h in heads]
        s_old = {x: sout_ref[x[0][0], x[1]] for x in cur}
        ws_qs = {x: _mm(jnp.concatenate([uw[x][:, GDN_DV:2 * GDN_DV], q[x] * eg[x]], axis=0), s_old[x])
                 for x in cur}
        delta = {x: uw[x][:, 0:GDN_DV] - ws_qs[x][0:c] for x in cur}
        for x in cur:
            (s, _), h = x
            hl = slice(h * LANES, (h + 1) * LANES)
            gch = gc[x[0]][:, hl]
            g_last = gch[c - 1:c, :]
            ks = k[x] * jnp.exp(g_last - gch)
            sout_ref[s, h] = s_old[x] * jnp.exp(g_last) + _mm_tn(ks, delta[x])
            o_h = ws_qs[x][c:2 * c] + _mm(qk[x], delta[x])
            r0 = ch * c
            o_ref[s, r0:r0 + c, hl] = _rms_gate(
                o_h, ng_ref[...], hd_ref[s, r0:r0 + c, GDN_CONV_DIM + h * GDN_DV:GDN_CONV_DIM + (h + 1) * GDN_DV])

    @pl.when(t_idx == pl.num_programs(1) - 1)
    def _():
        for s in range(n_seq):
            cbout_ref[s] = xwin_ref[s, tail:SUBLANES, :]


def _gdn_call(h_gdn, h_small, s0, conv_buf, conv_w, a_log, dt_bias, norm_g, *, seqs_per_step, chunks_per_step):
    bsz, t_len, _ = h_gdn.shape
    c = min(MAX_CHUNK, t_len)
    tb = c * chunks_per_step
    bb = seqs_per_step
    assert t_len % tb == 0 and bsz % bb == 0 and c >= SUBLANES and c & (c - 1) == 0
    sel = np.zeros((SMALL_COLS, 2 * GDN_V), np.float32)
    for h in range(GDN_HEADS):
        sel[SMALL_A0 + h, h * LANES:(h + 1) * LANES] = 1.0
        sel[SMALL_B0 + h, GDN_V + h * LANES:GDN_V + (h + 1) * LANES] = 1.0
    sel = jnp.asarray(sel, BF16)
    tri = jnp.asarray(np.tril(np.ones((c, c), np.float32)), BF16)
    alog_b = jnp.repeat(a_log.astype(F32), LANES)[None, :]
    dtb_b = jnp.repeat(dt_bias.astype(F32), LANES)[None, :]
    tok = lambda b, t: (b, t, 0)
    per_b4 = lambda b, t: (b, 0, 0, 0)
    per_b3 = lambda b, t: (b, 0, 0)
    kern = functools.partial(_gdn_kernel, c=c, n_seq=bb, n_chunks=chunks_per_step)
    return pl.pallas_call(
        kern,
        grid=(bsz // bb, t_len // tb),
        in_specs=[pl.BlockSpec((bb, tb, GDN_COLS), tok),
                  pl.BlockSpec((bb, tb, SMALL_COLS), tok),
                  pl.BlockSpec((bb, GDN_HEADS, GDN_DK, GDN_DV), per_b4),
                  pl.BlockSpec((bb, CONV_W - 1, GDN_CONV_DIM), per_b3),
                  _const_spec(conv_w.shape), _const_spec(sel.shape), _const_spec(alog_b.shape),
                  _const_spec(dtb_b.shape), _const_spec(norm_g.shape), _const_spec(tri.shape)],
        out_specs=[pl.BlockSpec((bb, tb, GDN_V), tok),
                   pl.BlockSpec((bb, GDN_HEADS, GDN_DK, GDN_DV), per_b4),
                   pl.BlockSpec((bb, CONV_W - 1, GDN_CONV_DIM), per_b3)],
        out_shape=[jax.ShapeDtypeStruct((bsz, t_len, GDN_V), F32),
                   jax.ShapeDtypeStruct((bsz, GDN_HEADS, GDN_DK, GDN_DV), F32),
                   jax.ShapeDtypeStruct((bsz, CONV_W - 1, GDN_CONV_DIM), F32)],
        scratch_shapes=[pltpu.VMEM((bb, SUBLANES + c, GDN_CONV_DIM), F32)],
        compiler_params=pltpu.CompilerParams(dimension_semantics=("arbitrary", "arbitrary"),
                                             vmem_limit_bytes=VMEM_LIMIT),
        name="gdn",
    )(h_gdn, h_small, s0, conv_buf, conv_w, sel, alog_b, dtb_b, norm_g, tri)


def _out_ffn_kernel(x_ref, og_ref, od_ref, wo_ref, g1_ref, b1_ref, wg_ref, wu_ref, wd_ref,
                    g2_ref, b2_ref, y_ref, hid_ref, *, alpha):
    m = (jnp.dot(og_ref[...].astype(BF16), wo_ref[0:GLA_V, :], preferred_element_type=F32)
         + jnp.dot(od_ref[...].astype(BF16), wo_ref[GLA_V:GLA_V + GDN_V, :], preferred_element_type=F32))
    x1 = _layer_norm(alpha * x_ref[...] + m, g1_ref[...], b1_ref[...])
    x1b = x1.astype(BF16)
    for f0 in range(0, D_FF, FF_TILE):
        gt = jnp.dot(x1b, wg_ref[:, f0:f0 + FF_TILE], preferred_element_type=F32)
        up = jnp.dot(x1b, wu_ref[:, f0:f0 + FF_TILE], preferred_element_type=F32)
        hid_ref[:, f0:f0 + FF_TILE] = (_silu(gt) * up).astype(BF16)
    f = jnp.dot(hid_ref[...], wd_ref[...], preferred_element_type=F32)
    y_ref[...] = _layer_norm(alpha * x1 + f, g2_ref[...], b2_ref[...])


def _out_ffn_call(x2d, o_gla, o_gdn, w_out, ln1_g, ln1_b, w_gate, w_up, w_down, ln2_g, ln2_b, *, alpha):
    n_tok = x2d.shape[0]
    tm = 512 if n_tok % 512 == 0 else n_tok
    row = lambda i: (i, 0)
    kern = functools.partial(_out_ffn_kernel, alpha=alpha)
    return pl.pallas_call(
        kern,
        grid=(n_tok // tm,),
        in_specs=[pl.BlockSpec((tm, D_MODEL), row), pl.BlockSpec((tm, GLA_V), row),
                  pl.BlockSpec((tm, GDN_V), row),
                  _const_spec(w_out.shape), _const_spec(ln1_g.shape), _const_spec(ln1_b.shape),
                  _const_spec(w_gate.shape), _const_spec(w_up.shape), _const_spec(w_down.shape),
                  _const_spec(ln2_g.shape), _const_spec(ln2_b.shape)],
        out_specs=pl.BlockSpec((tm, D_MODEL), row),
        out_shape=jax.ShapeDtypeStruct((n_tok, D_MODEL), F32),
        scratch_shapes=[pltpu.VMEM((tm, D_FF), BF16)],
        compiler_params=pltpu.CompilerParams(dimension_semantics=("arbitrary",),
                                             vmem_limit_bytes=VMEM_LIMIT),
        name="out_ffn",
    )(x2d, o_gla, o_gdn, w_out, ln1_g, ln1_b, w_gate, w_up, w_down, ln2_g, ln2_b)


def _regroup_w_in(w_in):
    o = 0
    q0 = o; o += GLA_QK
    k0 = o; o += GLA_QK
    v0 = o; o += GLA_V
    gg0 = o; o += GLA_V
    ga0 = o; o += GLA_GATE_RANK
    dqkv0 = o; o += GDN_CONV_DIM
    dg0 = o; o += GDN_V
    da0 = o; o += GDN_HEADS
    db0 = o; o += GDN_HEADS
    assert w_in.shape[1] == o and (q0, k0, v0, gg0) == (0, GLA_QK, 2 * GLA_QK, 2 * GLA_QK + GLA_V)
    w_gla = w_in[:, 0:ga0]
    w_gdn = w_in[:, dqkv0:da0]
    pad = jnp.zeros((w_in.shape[0], SMALL_COLS - GLA_GATE_RANK - 2 * GDN_HEADS), w_in.dtype)
    w_small = jnp.concatenate([w_in[:, ga0:dqkv0], w_in[:, da0:o], pad], axis=1)
    return w_gla.astype(BF16), w_gdn.astype(BF16), w_small.astype(BF16)


def _trunk_layer(x, s_gla, s_gdn, conv_buf, p, *, alpha):
    bsz, t_len, _ = x.shape
    x2d = x.reshape(bsz * t_len, D_MODEL)
    h_gla, h_gdn, h_small = _inproj_call(x2d, p["w_gla"], p["w_gdn"], p["w_small"])
    h_gla = h_gla.reshape(bsz, t_len, GLA_COLS)
    h_gdn = h_gdn.reshape(bsz, t_len, GDN_COLS)
    h_small = h_small.reshape(bsz, t_len, SMALL_COLS)
    n_chunks = max(1, t_len // MAX_CHUNK)
    cps = 4 if n_chunks % 4 == 0 else 1
    sps = 1 if cps > 1 else (4 if bsz % 4 == 0 else 1)
    o_gla, s_gla_new = _gla_call(h_gla, h_small, s_gla, p["wup_pad"], p["b_gate"], p["gla_norm_g"],
                                 seqs_per_step=sps, chunks_per_step=cps)
    o_gdn, s_gdn_new, buf_new = _gdn_call(h_gdn, h_small, s_gdn, conv_buf, p["conv_w"], p["a_log"],
                                          p["dt_bias"], p["gdn_norm_g"], seqs_per_step=sps,
                                          chunks_per_step=cps)
    y = _out_ffn_call(x2d, o_gla.reshape(bsz * t_len, GLA_V), o_gdn.reshape(bsz * t_len, GDN_V),
                      p["w_out"], p["ln1_g"], p["ln1_b"], p["w_gate"], p["w_up"], p["w_down"],
                      p["ln2_g"], p["ln2_b"], alpha=alpha)
    return y.reshape(bsz, t_len, D_MODEL), s_gla_new, s_gdn_new, buf_new


def kernel(x_prompt, x_sample, state_gla, state_gdn, state_gdn_conv, w_in, gla_w_gate_up, gla_b_gate,
           gla_norm_g, gdn_conv_w, gdn_a_log, gdn_dt_bias, gdn_norm_g, w_out, ln1_g, ln1_b,
           w_ffn_gate, w_ffn_up, w_ffn_down, ln2_g, ln2_b):
    depth = w_in.shape[0]
    alpha = float((2 * depth) ** 0.25)
    bp = x_prompt.shape[0]
    yp, ys = x_prompt, x_sample
    outs = [[] for _ in range(6)]
    for l in range(depth):
        w_gla, w_gdn, w_small = _regroup_w_in(w_in[l])
        wup_pad = jnp.zeros((SMALL_COLS, GLA_QK), F32).at[0:GLA_GATE_RANK].set(
            gla_w_gate_up[l].astype(F32)).astype(BF16)
        p = dict(w_gla=w_gla, w_gdn=w_gdn, w_small=w_small, wup_pad=wup_pad,
                 b_gate=gla_b_gate[l].astype(F32)[None, :], gla_norm_g=gla_norm_g[l].astype(F32)[None, :],
                 conv_w=gdn_conv_w[l].astype(F32), a_log=gdn_a_log[l], dt_bias=gdn_dt_bias[l],
                 gdn_norm_g=gdn_norm_g[l].astype(F32)[None, :], w_out=w_out[l].astype(BF16),
                 ln1_g=ln1_g[l][None, :], ln1_b=ln1_b[l][None, :], w_gate=w_ffn_gate[l].astype(BF16),
                 w_up=w_ffn_up[l].astype(BF16), w_down=w_ffn_down[l].astype(BF16),
                 ln2_g=ln2_g[l][None, :], ln2_b=ln2_b[l][None, :])
        z_gla = jnp.zeros((bp,) + state_gla.shape[2:], state_gla.dtype)
        z_gdn = jnp.zeros((bp,) + state_gdn.shape[2:], state_gdn.dtype)
        z_conv = jnp.zeros((bp,) + state_gdn_conv.shape[2:], state_gdn_conv.dtype)
        yp, a1, a2, a3 = _trunk_layer(yp, z_gla, z_gdn, z_conv, p, alpha=alpha)
        ys, b1, b2, b3 = _trunk_layer(ys, state_gla[l], state_gdn[l], state_gdn_conv[l], p, alpha=alpha)
        for lst, val in zip(outs, (a1, a2, a3, b1, b2, b3)):
            lst.append(val)
    return (yp, ys) + tuple(jnp.stack(o) for o in outs)
```

```python
import functools

import numpy as np
import jax
import jax.numpy as jnp
from jax import lax
from jax.experimental import pallas as pl
from jax.experimental.pallas import tpu as pltpu

F32 = jnp.float32
BF16 = jnp.bfloat16

D_MODEL = 1024
GLA_HEADS, GLA_DK, GLA_DV = 4, 64, 128
GLA_QK = GLA_HEADS * GLA_DK
GLA_V = GLA_HEADS * GLA_DV
GLA_GATE_RANK = 16
GLA_GATE_TAU = 16.0
GDN_HEADS, GDN_DK, GDN_DV = 4, 128, 128
GDN_QK = GDN_HEADS * GDN_DK
GDN_V = GDN_HEADS * GDN_DV
GDN_CONV_DIM = 2 * GDN_QK + GDN_V
CONV_W = 4
D_FF = 2816
NORM_EPS = 1e-5
L2_EPS = 1e-6

GLA_COLS = 2 * GLA_QK + 2 * GLA_V
GDN_COLS = GDN_CONV_DIM + GDN_V
SMALL_COLS = 128
SMALL_A0 = GLA_GATE_RANK
SMALL_B0 = GLA_GATE_RANK + GDN_HEADS

MAX_CHUNK = 64
LOG2_E = 1.4426950408889634
LANES = 128
SUBLANES = 8
VMEM_LIMIT = 56 * 1024 * 1024
FF_TILE = 256


def _mm(a, b):
    return jnp.dot(a.astype(BF16), b.astype(BF16), preferred_element_type=F32)


def _mm_nt(a, b):
    return lax.dot_general(a.astype(BF16), b.astype(BF16), (((1,), (1,)), ((), ())),
                           preferred_element_type=F32)


def _mm_tn(a, b):
    return lax.dot_general(a.astype(BF16), b.astype(BF16), (((0,), (0,)), ((), ())),
                           preferred_element_type=F32)


def _split3(x):
    hi = x.astype(BF16)
    r = x - hi.astype(F32)
    mid = r.astype(BF16)
    lo = (r - mid.astype(F32)).astype(BF16)
    return hi, mid, lo


def _sel_mm(c, x):
    hi, mid, lo = _split3(x)
    d = functools.partial(jnp.dot, preferred_element_type=F32)
    return (d(c, lo) + d(c, mid)) + d(c, hi)


def _mm_sel(x, c):
    hi, mid, lo = _split3(x)
    d = functools.partial(jnp.dot, preferred_element_type=F32)
    return (d(lo, c) + d(mid, c)) + d(hi, c)


def _sigmoid(x):
    return 1.0 / (1.0 + jnp.exp(-x))


def _silu(x):
    return x * _sigmoid(x)


def _softplus(x):
    return jnp.maximum(x, 0.0) + jnp.log1p(jnp.exp(-jnp.abs(x)))


def _log_sigmoid(x):
    return jnp.minimum(x, 0.0) - jnp.log1p(jnp.exp(-jnp.abs(x)))


def _layer_norm(x, g, b):
    mu = jnp.mean(x, axis=-1, keepdims=True)
    xc = x - mu
    var = jnp.mean(xc * xc, axis=-1, keepdims=True)
    return xc * lax.rsqrt(var + NORM_EPS) * g + b


def _rms_gate(o, g, gate):
    ms = jnp.mean(o * o, axis=-1, keepdims=True)
    return o * lax.rsqrt(ms + NORM_EPS) * g * _silu(gate)


def _const_spec(shape):
    nd = len(shape)
    return pl.BlockSpec(shape, lambda *_: (0,) * nd, pipeline_mode=pl.Buffered(1))


def _inproj_kernel(x_ref, wg_ref, wd_ref, ws_ref, og_ref, od_ref, os_ref):
    x = x_ref[...].astype(BF16)
    for w_ref, o_ref in ((wg_ref, og_ref), (wd_ref, od_ref), (ws_ref, os_ref)):
        n = w_ref.shape[1]
        step = min(n, 512)
        for c0 in range(0, n, step):
            o_ref[:, c0:c0 + step] = jnp.dot(x, w_ref[:, c0:c0 + step], preferred_element_type=F32)


def _inproj_call(x2d, w_gla, w_gdn, w_small):
    n_tok = x2d.shape[0]
    tm = 512 if n_tok % 512 == 0 else n_tok
    row = lambda i: (i, 0)
    return pl.pallas_call(
        _inproj_kernel,
        grid=(n_tok // tm,),
        in_specs=[pl.BlockSpec((tm, D_MODEL), row),
                  _const_spec(w_gla.shape), _const_spec(w_gdn.shape), _const_spec(w_small.shape)],
        out_specs=[pl.BlockSpec((tm, GLA_COLS), row), pl.BlockSpec((tm, GDN_COLS), row),
                   pl.BlockSpec((tm, SMALL_COLS), row)],
        out_shape=[jax.ShapeDtypeStruct((n_tok, GLA_COLS), F32),
                   jax.ShapeDtypeStruct((n_tok, GDN_COLS), F32),
                   jax.ShapeDtypeStruct((n_tok, SMALL_COLS), F32)],
        compiler_params=pltpu.CompilerParams(dimension_semantics=("arbitrary",),
                                             vmem_limit_bytes=VMEM_LIMIT),
        name="inproj",
    )(x2d, w_gla, w_gdn, w_small)


def _gla_exponent_matrix(c):
    i = np.arange(c)[:, None]
    t = np.arange(c)[None, :]
    mats = [(t <= i).astype(np.float32)]
    for size in _gla_level_sizes(c):
        if size // 2 < SUBLANES:
            m = (i // size) * size + size // 2
            mats.append(((t > m) & (t <= i)).astype(np.float32) - ((t > i) & (t <= m)).astype(np.float32))
    return np.concatenate(mats, axis=0)


def _gla_level_sizes(c):
    sizes, size = [], c
    while size >= 2:
        sizes.append(size)
        size //= 2
    return sizes


def _gla_kernel(hg_ref, hs_ref, s0_ref, wup_ref, bg_ref, ng_ref, cm_ref,
                o_ref, sout_ref, sbd_ref, *, c, n_seq, n_chunks):
    t_idx = pl.program_id(1)
    pair = lambda p: slice(p * LANES, (p + 1) * LANES)

    @pl.when(t_idx == 0)
    def _():
        sbd_ref[...] = jnp.zeros_like(sbd_ref)
        for s in range(n_seq):
            for p in range(2):
                sbd_ref[s, p, 0:GLA_DK, 0:GLA_DV] = s0_ref[s, 2 * p]
                sbd_ref[s, p, GLA_DK:2 * GLA_DK, GLA_DV:2 * GLA_DV] = s0_ref[s, 2 * p + 1]

    lane = lax.broadcasted_iota(jnp.int32, (1, GLA_QK), 1)
    col_mask = [(jnp.bitwise_and(lax.shift_right_logical(lane, 6), 1) == g).astype(F32) for g in range(2)]
    r_bd = lax.shift_right_logical(lax.broadcasted_iota(jnp.int32, (2 * GLA_DK, 2 * GLA_DV), 0), 6)
    c_bd = lax.shift_right_logical(lax.broadcasted_iota(jnp.int32, (2 * GLA_DK, 2 * GLA_DV), 1), 7)
    mask_bd = (r_bd == c_bd).astype(F32)
    eye128 = (lax.broadcasted_iota(jnp.int32, (LANES, LANES), 0)
              == lax.broadcasted_iota(jnp.int32, (LANES, LANES), 1))
    ri = jnp.bitwise_and(lax.broadcasted_iota(jnp.int32, (2 * c, 2 * c), 0), c - 1)
    ci = jnp.bitwise_and(lax.broadcasted_iota(jnp.int32, (2 * c, 2 * c), 1), c - 1)
    pair_level = jnp.where(ri >= ci, lax.clz(jnp.bitwise_xor(ri, ci)), -1)
    zero_q = jnp.zeros((c, LANES), F32)
    zero_v = jnp.zeros((c, GLA_DV), F32)

    units = [(s, ch) for s in range(n_seq) for ch in range(n_chunks)]
    rows = {u: slice(u[1] * c, (u[1] + 1) * c) for u in units}

    z = {u: _mm(hs_ref[u[0], rows[u], :], wup_ref[...]) + bg_ref[...] for u in units}
    la = {u: _log_sigmoid(z[u]) * (LOG2_E / GLA_GATE_TAU) for u in units}
    ex = {u: _sel_mm(cm_ref[...], la[u]) for u in units}
    b = {u: ex[u][0:c] for u in units}
    q = {u: hg_ref[u[0], rows[u], 0:GLA_QK] * (GLA_DK ** -0.5) for u in units}
    k = {u: hg_ref[u[0], rows[u], GLA_QK:2 * GLA_QK] for u in units}
    kg = {u: [k[u] * col_mask[g] for g in range(2)] for u in units}

    def level_weight(u, size, n_fine):
        if size // 2 < SUBLANES:
            d = ex[u][(1 + n_fine) * c:(2 + n_fine) * c]
        else:
            b_mid = jnp.concatenate([jnp.broadcast_to(b[u][m:m + 1, :], (size, GLA_QK))
                                     for m in range(size // 2, c, size)], axis=0)
            d = b[u] - b_mid
        return jnp.exp2(-jnp.abs(d))

    def stacked_scores(qw, kw0, kw1):
        lhs = jnp.concatenate([jnp.concatenate([qw[:, pair(0)], zero_q], axis=1),
                               jnp.concatenate([zero_q, qw[:, pair(1)]], axis=1)], axis=0)
        return _mm_nt(lhs, jnp.concatenate([kw0, kw1], axis=0))

    score = {u: jnp.zeros((2 * c, 2 * c), F32) for u in units}
    n_fine = 0
    for size in _gla_level_sizes(c):
        code = 31 - (size.bit_length() - 2)
        for u in units:
            w = level_weight(u, size, n_fine)
            a = stacked_scores(q[u] * w, kg[u][0] * w, kg[u][1] * w)
            score[u] = jnp.where(pair_level == code, a, score[u])
        if size // 2 < SUBLANES:
            n_fine += 1
    for u in units:
        a = stacked_scores(q[u], kg[u][0], kg[u][1])
        score[u] = jnp.where(pair_level == 32, a, score[u])

    v = {u: hg_ref[u[0], rows[u], 2 * GLA_QK:2 * GLA_QK + GLA_V] for u in units}
    o_intra = {}
    for u in units:
        for a in range(2):
            v_bd = jnp.concatenate(
                [jnp.concatenate([v[u][:, 2 * a * GLA_DV:(2 * a + 1) * GLA_DV], zero_v], axis=1),
                 jnp.concatenate([zero_v, v[u][:, (2 * a + 1) * GLA_DV:(2 * a + 2) * GLA_DV]], axis=1)], axis=0)
            o_pair = _mm(score[u][a * c:(a + 1) * c], v_bd)
            o_intra[u, 2 * a] = o_pair[:, 0:GLA_DV]
            o_intra[u, 2 * a + 1] = o_pair[:, GLA_DV:2 * GLA_DV]

    qhat = {u: q[u] * jnp.exp2(b[u]) for u in units}
    upd, decay = {}, {}
    for u in units:
        b_last = b[u][c - 1:c, :]
        ks = k[u] * jnp.exp2(b_last - b[u])
        for p in range(2):
            upd[u, p] = _mm_tn(ks[:, pair(p)], v[u][:, 2 * p * GLA_DV:(2 * p + 2) * GLA_DV]) * mask_bd
            bl = jnp.broadcast_to(b_last[:, pair(p)], (LANES, LANES))
            decay[u, p] = jnp.exp2(jnp.sum(jnp.where(eye128, bl, 0.0), axis=1, keepdims=True))

    for ch in range(n_chunks):
        for s in range(n_seq):
            u = (s, ch)
            for p in range(2):
                s_old = sbd_ref[s, p]
                inter = _mm(qhat[u][:, pair(p)], s_old)
                sbd_ref[s, p] = s_old * decay[u, p] + upd[u, p]
                for hh in range(2):
                    h = 2 * p + hh
                    hv = slice(h * GLA_DV, (h + 1) * GLA_DV)
                    o_h = inter[:, hh * GLA_DV:(hh + 1) * GLA_DV] + o_intra[u, h]
                    gate = hg_ref[s, rows[u], 2 * GLA_QK + GLA_V + h * GLA_DV:2 * GLA_QK + GLA_V + (h + 1) * GLA_DV]
                    o_ref[s, rows[u], hv] = _rms_gate(o_h, ng_ref[...], gate)

    @pl.when(t_idx == pl.num_programs(1) - 1)
    def _():
        for s in range(n_seq):
            for p in range(2):
                sout_ref[s, 2 * p] = sbd_ref[s, p, 0:GLA_DK, 0:GLA_DV]
                sout_ref[s, 2 * p + 1] = sbd_ref[s, p, GLA_DK:2 * GLA_DK, GLA_DV:2 * GLA_DV]


def _gla_call(h_gla, h_small, s0, wup_pad, b_gate, norm_g, *, seqs_per_step, chunks_per_step):
    bsz, t_len, _ = h_gla.shape
    c = min(MAX_CHUNK, t_len)
    tb = c * chunks_per_step
    bb = seqs_per_step
    assert t_len % tb == 0 and bsz % bb == 0 and c >= SUBLANES and c & (c - 1) == 0
    cm = jnp.asarray(_gla_exponent_matrix(c), BF16)
    tok = lambda b, t: (b, t, 0)
    per_b = lambda b, t: (b, 0, 0, 0)
    kern = functools.partial(_gla_kernel, c=c, n_seq=bb, n_chunks=chunks_per_step)
    return pl.pallas_call(
        kern,
        grid=(bsz // bb, t_len // tb),
        in_specs=[pl.BlockSpec((bb, tb, GLA_COLS), tok),
                  pl.BlockSpec((bb, tb, SMALL_COLS), tok),
                  pl.BlockSpec((bb, GLA_HEADS, GLA_DK, GLA_DV), per_b),
                  _const_spec(wup_pad.shape), _const_spec(b_gate.shape), _const_spec(norm_g.shape),
                  _const_spec(cm.shape)],
        out_specs=[pl.BlockSpec((bb, tb, GLA_V), tok),
                   pl.BlockSpec((bb, GLA_HEADS, GLA_DK, GLA_DV), per_b)],
        out_shape=[jax.ShapeDtypeStruct((bsz, t_len, GLA_V), F32),
                   jax.ShapeDtypeStruct((bsz, GLA_HEADS, GLA_DK, GLA_DV), F32)],
        scratch_shapes=[pltpu.VMEM((bb, 2, 2 * GLA_DK, 2 * GLA_DV), F32)],
        compiler_params=pltpu.CompilerParams(dimension_semantics=("arbitrary", "arbitrary"),
                                             vmem_limit_bytes=VMEM_LIMIT),
        name="gla",
    )(h_gla, h_small, s0, wup_pad, b_gate, norm_g, cm)


def _gdn_kernel(hd_ref, hs_ref, s0_ref, cb_ref, cw_ref, sel_ref, alog_ref, dtb_ref, ng_ref, tri_ref,
                o_ref, sout_ref, cbout_ref, xwin_ref, *, c, n_seq, n_chunks):
    t_idx = pl.program_id(1)
    tail = SUBLANES - (CONV_W - 1)
    tb = c * n_chunks

    @pl.when(t_idx == 0)
    def _():
        sout_ref[...] = s0_ref[...]
        for s in range(n_seq):
            xwin_ref[s, 0:SUBLANES, :] = jnp.zeros((SUBLANES, GDN_CONV_DIM), F32)
            xwin_ref[s, tail:SUBLANES, :] = cb_ref[s]

    ri = lax.broadcasted_iota(jnp.int32, (c, c), 0)
    ci = lax.broadcasted_iota(jnp.int32, (c, c), 1)
    eye = ri == ci
    causal = ri >= ci
    strict = ri > ci
    n_pow = c.bit_length() - 2
    units = [(s, ch) for s in range(n_seq) for ch in range(n_chunks)]
    heads = range(GDN_HEADS)
    uh = [(u, h) for u in units for h in heads]

    conv = {}
    for s in range(n_seq):
        xwin_ref[s, SUBLANES:SUBLANES + c, :] = hd_ref[s, 0:c, 0:GDN_CONV_DIM]
        for ch in range(n_chunks):
            r0 = ch * c
            acc = None
            for i in range(CONV_W):
                if ch == 0:
                    rows = xwin_ref[s, tail + i:tail + i + c, :]
                else:
                    rows = hd_ref[s, r0 - (CONV_W - 1) + i:r0 - (CONV_W - 1) + i + c, 0:GDN_CONV_DIM]
                term = rows * cw_ref[i:i + 1, :]
                acc = term if acc is None else acc + term
            conv[(s, ch)] = _silu(acc)
        xwin_ref[s, 0:SUBLANES, :] = hd_ref[s, tb - SUBLANES:tb, 0:GDN_CONV_DIM]

    lane_s = lax.broadcasted_iota(jnp.int32, (1, SMALL_COLS), 1)
    is_a = (lane_s >= SMALL_A0) & (lane_s < SMALL_A0 + GDN_HEADS)
    hs = {u: hs_ref[u[0], u[1] * c:(u[1] + 1) * c, :] for u in units}
    g_s = {u: -jnp.exp(alog_ref[...]) * _softplus(hs[u] + dtb_ref[...]) for u in units}
    gc_s = {u: _sel_mm(tri_ref[...], g_s[u]) for u in units}
    bc = {u: _mm_sel(jnp.where(is_a, gc_s[u], _sigmoid(hs[u])), sel_ref[...]) for u in units}
    gc = {u: bc[u][:, 0:GDN_V] for u in units}
    beta = {u: bc[u][:, GDN_V:2 * GDN_V] for u in units}

    q, k, gam, eg, bt = {}, {}, {}, {}, {}
    for (u, h) in uh:
        cv = conv[u]
        qh = cv[:, h * GDN_DK:(h + 1) * GDN_DK]
        kh = cv[:, GDN_QK + h * GDN_DK:GDN_QK + (h + 1) * GDN_DK]
        q[u, h] = qh * lax.rsqrt(jnp.sum(qh * qh, axis=-1, keepdims=True) + L2_EPS) * (GDN_DK ** -0.5)
        k[u, h] = kh * lax.rsqrt(jnp.sum(kh * kh, axis=-1, keepdims=True) + L2_EPS)
        gch = gc[u][:, h * LANES:(h + 1) * LANES]
        gcol = gch[:, 0:c]
        grow = jnp.sum(jnp.where(eye, gcol, 0.0), axis=0, keepdims=True)
        gam[u, h] = jnp.where(causal, jnp.exp(jnp.minimum(gcol - grow, 0.0)), 0.0)
        eg[u, h] = jnp.exp(gch)
        bt[u, h] = beta[u][:, h * LANES:(h + 1) * LANES]

    qkk = {x: _mm_nt(jnp.concatenate([q[x], k[x]], axis=0), k[x]) for x in uh}
    qk = {x: jnp.where(causal, qkk[x][0:c] * gam[x], 0.0) for x in uh}
    n_mat = {x: jnp.where(strict, bt[x][:, 0:c] * qkk[x][c:2 * c] * gam[x], 0.0) for x in uh}

    t_m = {x: -n_mat[x] for x in uh}
    x_pow = {x: _mm(n_mat[x], n_mat[x]) for x in uh}
    for _ in range(n_pow - 1):
        both = {x: _mm(jnp.concatenate([x_pow[x], t_m[x]], axis=0), x_pow[x]) for x in uh}
        t_m = {x: t_m[x] + x_pow[x] + both[x][c:2 * c] for x in uh}
        x_pow = {x: both[x][0:c] for x in uh}
    last = {x: _mm(t_m[x], x_pow[x]) for x in uh}
    t_m = {x: t_m[x] + x_pow[x] + last[x] for x in uh}

    rhs = {}
    for (u, h) in uh:
        vh = conv[u][:, 2 * GDN_QK + h * GDN_DV:2 * GDN_QK + (h + 1) * GDN_DV]
        rhs[u, h] = jnp.concatenate([bt[u, h] * vh, bt[u, h] * eg[u, h] * k[u, h]], axis=1)
    uw = {x: rhs[x] + _mm(t_m[x], rhs[x]) for x in uh}

    for ch in range(n_chunks):
        cur = [((s, ch), h) for s in range(n_seq) for h in heads]
        s_old = {x: sout_ref[x[0][0], x[1]] for x in cur}
        ws_qs = {x: _mm(jnp.concatenate([uw[x][:, GDN_DV:2 * GDN_DV], q[x] * eg[x]], axis=0), s_old[x])
                 for x in cur}
        delta = {x: uw[x][:, 0:GDN_DV] - ws_qs[x][0:c] for x in cur}
        for x in cur:
            (s, _), h = x
            hl = slice(h * LANES, (h + 1) * LANES)
            gch = gc[x[0]][:, hl]
            g_last = gch[c - 1:c, :]
            ks = k[x] * jnp.exp(g_last - gch)
            sout_ref[s, h] = s_old[x] * jnp.exp(g_last) + _mm_tn(ks, delta[x])
            o_h = ws_qs[x][c:2 * c] + _mm(qk[x], delta[x])
            r0 = ch * c
            o_ref[s, r0:r0 + c, hl] = _rms_gate(
                o_h, ng_ref[...], hd_ref[s, r0:r0 + c, GDN_CONV_DIM + h * GDN_DV:GDN_CONV_DIM + (h + 1) * GDN_DV])

    @pl.when(t_idx == pl.num_programs(1) - 1)
    def _():
        for s in range(n_seq):
            cbout_ref[s] = xwin_ref[s, tail:SUBLANES, :]


def _gdn_call(h_gdn, h_small, s0, conv_buf, conv_w, a_log, dt_bias, norm_g, *, seqs_per_step, chunks_per_step):
    bsz, t_len, _ = h_gdn.shape
    c = min(MAX_CHUNK, t_len)
    tb = c * chunks_per_step
    bb = seqs_per_step
    assert t_len % tb == 0 and bsz % bb == 0 and c >= SUBLANES and c & (c - 1) == 0
    sel = np.zeros((SMALL_COLS, 2 * GDN_V), np.float32)
    for h in range(GDN_HEADS):
        sel[SMALL_A0 + h, h * LANES:(h + 1) * LANES] = 1.0
        sel[SMALL_B0 + h, GDN_V + h * LANES:GDN_V + (h + 1) * LANES] = 1.0
    sel = jnp.asarray(sel, BF16)
    tri = jnp.asarray(np.tril(np.ones((c, c), np.float32)), BF16)
    alog_b = jnp.zeros((1, SMALL_COLS), F32).at[0, SMALL_A0:SMALL_A0 + GDN_HEADS].set(a_log.astype(F32))
    dtb_b = jnp.zeros((1, SMALL_COLS), F32).at[0, SMALL_A0:SMALL_A0 + GDN_HEADS].set(dt_bias.astype(F32))
    tok = lambda b, t: (b, t, 0)
    per_b4 = lambda b, t: (b, 0, 0, 0)
    per_b3 = lambda b, t: (b, 0, 0)
    kern = functools.partial(_gdn_kernel, c=c, n_seq=bb, n_chunks=chunks_per_step)
    return pl.pallas_call(
        kern,
        grid=(bsz // bb, t_len // tb),
        in_specs=[pl.BlockSpec((bb, tb, GDN_COLS), tok),
                  pl.BlockSpec((bb, tb, SMALL_COLS), tok),
                  pl.BlockSpec((bb, GDN_HEADS, GDN_DK, GDN_DV), per_b4),
                  pl.BlockSpec((bb, CONV_W - 1, GDN_CONV_DIM), per_b3),
                  _const_spec(conv_w.shape), _const_spec(sel.shape), _const_spec(alog_b.shape),
                  _const_spec(dtb_b.shape), _const_spec(norm_g.shape), _const_spec(tri.shape)],
        out_specs=[pl.BlockSpec((bb, tb, GDN_V), tok),
                   pl.BlockSpec((bb, GDN_HEADS, GDN_DK, GDN_DV), per_b4),
                   pl.BlockSpec((bb, CONV_W - 1, GDN_CONV_DIM), per_b3)],
        out_shape=[jax.ShapeDtypeStruct((bsz, t_len, GDN_V), F32),
                   jax.ShapeDtypeStruct((bsz, GDN_HEADS, GDN_DK, GDN_DV), F32),
                   jax.ShapeDtypeStruct((bsz, CONV_W - 1, GDN_CONV_DIM), F32)],
        scratch_shapes=[pltpu.VMEM((bb, SUBLANES + c, GDN_CONV_DIM), F32)],
        compiler_params=pltpu.CompilerParams(dimension_semantics=("arbitrary", "arbitrary"),
                                             vmem_limit_bytes=VMEM_LIMIT),
        name="gdn",
    )(h_gdn, h_small, s0, conv_buf, conv_w, sel, alog_b, dtb_b, norm_g, tri)


def _out_ffn_kernel(x_ref, og_ref, od_ref, wo_ref, g1_ref, b1_ref, wg_ref, wu_ref, wd_ref,
                    g2_ref, b2_ref, y_ref, hid_ref, *, alpha):
    m = (jnp.dot(og_ref[...].astype(BF16), wo_ref[0:GLA_V, :], preferred_element_type=F32)
         + jnp.dot(od_ref[...].astype(BF16), wo_ref[GLA_V:GLA_V + GDN_V, :], preferred_element_type=F32))
    x1 = _layer_norm(alpha * x_ref[...] + m, g1_ref[...], b1_ref[...])
    x1b = x1.astype(BF16)
    for f0 in range(0, D_FF, FF_TILE):
        gt = jnp.dot(x1b, wg_ref[:, f0:f0 + FF_TILE], preferred_element_type=F32)
        up = jnp.dot(x1b, wu_ref[:, f0:f0 + FF_TILE], preferred_element_type=F32)
        hid_ref[:, f0:f0 + FF_TILE] = (_silu(gt) * up).astype(BF16)
    f = jnp.dot(hid_ref[...], wd_ref[...], preferred_element_type=F32)
    y_ref[...] = _layer_norm(alpha * x1 + f, g2_ref[...], b2_ref[...])


def _out_ffn_call(x2d, o_gla, o_gdn, w_out, ln1_g, ln1_b, w_gate, w_up, w_down, ln2_g, ln2_b, *, alpha):
    n_tok = x2d.shape[0]
    tm = 512 if n_tok % 512 == 0 else n_tok
    row = lambda i: (i, 0)
    kern = functools.partial(_out_ffn_kernel, alpha=alpha)
    return pl.pallas_call(
        kern,
        grid=(n_tok // tm,),
        in_specs=[pl.BlockSpec((tm, D_MODEL), row), pl.BlockSpec((tm, GLA_V), row),
                  pl.BlockSpec((tm, GDN_V), row),
                  _const_spec(w_out.shape), _const_spec(ln1_g.shape), _const_spec(ln1_b.shape),
                  _const_spec(w_gate.shape), _const_spec(w_up.shape), _const_spec(w_down.shape),
                  _const_spec(ln2_g.shape), _const_spec(ln2_b.shape)],
        out_specs=pl.BlockSpec((tm, D_MODEL), row),
        out_shape=jax.ShapeDtypeStruct((n_tok, D_MODEL), F32),
        scratch_shapes=[pltpu.VMEM((tm, D_FF), BF16)],
        compiler_params=pltpu.CompilerParams(dimension_semantics=("arbitrary",),
                                             vmem_limit_bytes=VMEM_LIMIT),
        name="out_ffn",
    )(x2d, o_gla, o_gdn, w_out, ln1_g, ln1_b, w_gate, w_up, w_down, ln2_g, ln2_b)


def _regroup_w_in(w_in):
    o = 0
    q0 = o; o += GLA_QK
    k0 = o; o += GLA_QK
    v0 = o; o += GLA_V
    gg0 = o; o += GLA_V
    ga0 = o; o += GLA_GATE_RANK
    dqkv0 = o; o += GDN_CONV_DIM
    dg0 = o; o += GDN_V
    da0 = o; o += GDN_HEADS
    db0 = o; o += GDN_HEADS
    assert w_in.shape[1] == o and (q0, k0, v0, gg0) == (0, GLA_QK, 2 * GLA_QK, 2 * GLA_QK + GLA_V)
    w_gla = w_in[:, 0:ga0]
    w_gdn = w_in[:, dqkv0:da0]
    pad = jnp.zeros((w_in.shape[0], SMALL_COLS - GLA_GATE_RANK - 2 * GDN_HEADS), w_in.dtype)
    w_small = jnp.concatenate([w_in[:, ga0:dqkv0], w_in[:, da0:o], pad], axis=1)
    return w_gla.astype(BF16), w_gdn.astype(BF16), w_small.astype(BF16)


def _trunk_layer(x, s_gla, s_gdn, conv_buf, p, *, alpha):
    bsz, t_len, _ = x.shape
    x2d = x.reshape(bsz * t_len, D_MODEL)
    h_gla, h_gdn, h_small = _inproj_call(x2d, p["w_gla"], p["w_gdn"], p["w_small"])
    h_gla = h_gla.reshape(bsz, t_len, GLA_COLS)
    h_gdn = h_gdn.reshape(bsz, t_len, GDN_COLS)
    h_small = h_small.reshape(bsz, t_len, SMALL_COLS)
    n_chunks = max(1, t_len // MAX_CHUNK)
    cps = 4 if n_chunks % 4 == 0 else 1
    sps = 1 if cps > 1 else (4 if bsz % 4 == 0 else 1)
    o_gla, s_gla_new = _gla_call(h_gla, h_small, s_gla, p["wup_pad"], p["b_gate"], p["gla_norm_g"],
                                 seqs_per_step=sps, chunks_per_step=cps)
    o_gdn, s_gdn_new, buf_new = _gdn_call(h_gdn, h_small, s_gdn, conv_buf, p["conv_w"], p["a_log"],
                                          p["dt_bias"], p["gdn_norm_g"], seqs_per_step=sps,
                                          chunks_per_step=cps)
    y = _out_ffn_call(x2d, o_gla.reshape(bsz * t_len, GLA_V), o_gdn.reshape(bsz * t_len, GDN_V),
                      p["w_out"], p["ln1_g"], p["ln1_b"], p["w_gate"], p["w_up"], p["w_down"],
                      p["ln2_g"], p["ln2_b"], alpha=alpha)
    return y.reshape(bsz, t_len, D_MODEL), s_gla_new, s_gdn_new, buf_new


def kernel(x_prompt, x_sample, state_gla, state_gdn, state_gdn_conv, w_in, gla_w_gate_up, gla_b_gate,
           gla_norm_g, gdn_conv_w, gdn_a_log, gdn_dt_bias, gdn_norm_g, w_out, ln1_g, ln1_b,
           w_ffn_gate, w_ffn_up, w_ffn_down, ln2_g, ln2_b):
    depth = w_in.shape[0]
    alpha = float((2 * depth) ** 0.25)
    bp = x_prompt.shape[0]
    yp, ys = x_prompt, x_sample
    outs = [[] for _ in range(6)]
    for l in range(depth):
        w_gla, w_gdn, w_small = _regroup_w_in(w_in[l])
        wup_pad = jnp.zeros((SMALL_COLS, GLA_QK), F32).at[0:GLA_GATE_RANK].set(
            gla_w_gate_up[l].astype(F32)).astype(BF16)
        p = dict(w_gla=w_gla, w_gdn=w_gdn, w_small=w_small, wup_pad=wup_pad,
                 b_gate=gla_b_gate[l].astype(F32)[None, :], gla_norm_g=gla_norm_g[l].astype(F32)[None, :],
                 conv_w=gdn_conv_w[l].astype(F32), a_log=gdn_a_log[l], dt_bias=gdn_dt_bias[l],
                 gdn_norm_g=gdn_norm_g[l].astype(F32)[None, :], w_out=w_out[l].astype(BF16),
                 ln1_g=ln1_g[l][None, :], ln1_b=ln1_b[l][None, :], w_gate=w_ffn_gate[l].astype(BF16),
                 w_up=w_ffn_up[l].astype(BF16), w_down=w_ffn_down[l].astype(BF16),
                 ln2_g=ln2_g[l][None, :], ln2_b=ln2_b[l][None, :])
        z_gla = jnp.zeros((bp,) + state_gla.shape[2:], state_gla.dtype)
        z_gdn = jnp.zeros((bp,) + state_gdn.shape[2:], state_gdn.dtype)
        z_conv = jnp.zeros((bp,) + state_gdn_conv.shape[2:], state_gdn_conv.dtype)
        yp, a1, a2, a3 = _trunk_layer(yp, z_gla, z_gdn, z_conv, p, alpha=alpha)
        ys, b1, b2, b3 = _trunk_layer(ys, state_gla[l], state_gdn[l], state_gdn_conv[l], p, alpha=alpha)
        for lst, val in zip(outs, (a1, a2, a3, b1, b2, b3)):
            lst.append(val)
    return (yp, ys) + tuple(jnp.stack(o) for o in outs)
```

```python
import functools

import numpy as np
import jax
import jax.numpy as jnp
from jax import lax
from jax.experimental import pallas as pl
from jax.experimental.pallas import tpu as pltpu

F32 = jnp.float32
BF16 = jnp.bfloat16

D_MODEL = 1024
GLA_HEADS, GLA_DK, GLA_DV = 4, 64, 128
GLA_QK = GLA_HEADS * GLA_DK
GLA_V = GLA_HEADS * GLA_DV
GLA_GATE_RANK = 16
GLA_GATE_TAU = 16.0
GDN_HEADS, GDN_DK, GDN_DV = 4, 128, 128
GDN_QK = GDN_HEADS * GDN_DK
GDN_V = GDN_HEADS * GDN_DV
GDN_CONV_DIM = 2 * GDN_QK + GDN_V
CONV_W = 4
D_FF = 2816
NORM_EPS = 1e-5
L2_EPS = 1e-6

GLA_COLS = 2 * GLA_QK + 2 * GLA_V
GDN_COLS = GDN_CONV_DIM + GDN_V
SMALL_COLS = 128
SMALL_A0 = GLA_GATE_RANK
SMALL_B0 = GLA_GATE_RANK + GDN_HEADS

MAX_CHUNK = 64
LOG2_E = 1.4426950408889634
LANES = 128
SUBLANES = 8
VMEM_LIMIT = 56 * 1024 * 1024
FF_TILE = 256


def _mm(a, b):
    return jnp.dot(a.astype(BF16), b.astype(BF16), preferred_element_type=F32)


def _mm_nt(a, b):
    return lax.dot_general(a.astype(BF16), b.astype(BF16), (((1,), (1,)), ((), ())),
                           preferred_element_type=F32)


def _mm_tn(a, b):
    return lax.dot_general(a.astype(BF16), b.astype(BF16), (((0,), (0,)), ((), ())),
                           preferred_element_type=F32)


def _split3(x):
    hi = x.astype(BF16)
    r = x - hi.astype(F32)
    mid = r.astype(BF16)
    lo = (r - mid.astype(F32)).astype(BF16)
    return hi, mid, lo


def _sel_mm(c, x):
    hi, mid, lo = _split3(x)
    d = functools.partial(jnp.dot, preferred_element_type=F32)
    return (d(c, lo) + d(c, mid)) + d(c, hi)


def _mm_sel(x, c):
    hi, mid, lo = _split3(x)
    d = functools.partial(jnp.dot, preferred_element_type=F32)
    return (d(lo, c) + d(mid, c)) + d(hi, c)


def _sigmoid(x):
    return 1.0 / (1.0 + jnp.exp(-x))


def _silu(x):
    return x * _sigmoid(x)


def _softplus(x):
    return jnp.maximum(x, 0.0) + jnp.log1p(jnp.exp(-jnp.abs(x)))


def _log_sigmoid(x):
    return jnp.minimum(x, 0.0) - jnp.log1p(jnp.exp(-jnp.abs(x)))


def _layer_norm(x, g, b):
    mu = jnp.mean(x, axis=-1, keepdims=True)
    xc = x - mu
    var = jnp.mean(xc * xc, axis=-1, keepdims=True)
    return xc * lax.rsqrt(var + NORM_EPS) * g + b


def _rms_gate(o, g, gate):
    ms = jnp.mean(o * o, axis=-1, keepdims=True)
    return o * lax.rsqrt(ms + NORM_EPS) * g * _silu(gate)


def _const_spec(shape):
    nd = len(shape)
    return pl.BlockSpec(shape, lambda *_: (0,) * nd, pipeline_mode=pl.Buffered(1))


def _inproj_kernel(x_ref, wg_ref, wd_ref, ws_ref, og_ref, od_ref, os_ref):
    x = x_ref[...].astype(BF16)
    for w_ref, o_ref in ((wg_ref, og_ref), (wd_ref, od_ref), (ws_ref, os_ref)):
        n = w_ref.shape[1]
        step = min(n, 512)
        for c0 in range(0, n, step):
            o_ref[:, c0:c0 + step] = jnp.dot(x, w_ref[:, c0:c0 + step], preferred_element_type=F32)


def _inproj_call(x2d, w_gla, w_gdn, w_small):
    n_tok = x2d.shape[0]
    tm = 512 if n_tok % 512 == 0 else n_tok
    row = lambda i: (i, 0)
    return pl.pallas_call(
        _inproj_kernel,
        grid=(n_tok // tm,),
        in_specs=[pl.BlockSpec((tm, D_MODEL), row),
                  _const_spec(w_gla.shape), _const_spec(w_gdn.shape), _const_spec(w_small.shape)],
        out_specs=[pl.BlockSpec((tm, GLA_COLS), row), pl.BlockSpec((tm, GDN_COLS), row),
                   pl.BlockSpec((tm, SMALL_COLS), row)],
        out_shape=[jax.ShapeDtypeStruct((n_tok, GLA_COLS), F32),
                   jax.ShapeDtypeStruct((n_tok, GDN_COLS), F32),
                   jax.ShapeDtypeStruct((n_tok, SMALL_COLS), F32)],
        compiler_params=pltpu.CompilerParams(dimension_semantics=("arbitrary",),
                                             vmem_limit_bytes=VMEM_LIMIT),
        name="inproj",
    )(x2d, w_gla, w_gdn, w_small)


def _gla_exponent_matrix(c):
    i = np.arange(c)[:, None]
    t = np.arange(c)[None, :]
    mats = [(t <= i).astype(np.float32)]
    for size in _gla_level_sizes(c):
        if size // 2 < SUBLANES:
            m = (i // size) * size + size // 2
            mats.append(((t > m) & (t <= i)).astype(np.float32) - ((t > i) & (t <= m)).astype(np.float32))
    return np.concatenate(mats, axis=0)


def _gla_level_sizes(c):
    sizes, size = [], c
    while size >= 2:
        sizes.append(size)
        size //= 2
    return sizes


def _gla_kernel(hg_ref, hs_ref, s0_ref, wup_ref, bg_ref, cm_ref,
                o_ref, sout_ref, sbd_ref, *, c, n_seq, n_chunks):
    t_idx = pl.program_id(1)
    pair = lambda p: slice(p * LANES, (p + 1) * LANES)

    @pl.when(t_idx == 0)
    def _():
        sbd_ref[...] = jnp.zeros_like(sbd_ref)
        for s in range(n_seq):
            for p in range(2):
                sbd_ref[s, p, 0:GLA_DK, 0:GLA_DV] = s0_ref[s, 2 * p]
                sbd_ref[s, p, GLA_DK:2 * GLA_DK, GLA_DV:2 * GLA_DV] = s0_ref[s, 2 * p + 1]

    lane = lax.broadcasted_iota(jnp.int32, (1, GLA_QK), 1)
    col_mask = [(jnp.bitwise_and(lax.shift_right_logical(lane, 6), 1) == g).astype(F32) for g in range(2)]
    r_bd = lax.shift_right_logical(lax.broadcasted_iota(jnp.int32, (2 * GLA_DK, 2 * GLA_DV), 0), 6)
    c_bd = lax.shift_right_logical(lax.broadcasted_iota(jnp.int32, (2 * GLA_DK, 2 * GLA_DV), 1), 7)
    mask_bd = (r_bd == c_bd).astype(F32)
    eye128 = (lax.broadcasted_iota(jnp.int32, (LANES, LANES), 0)
              == lax.broadcasted_iota(jnp.int32, (LANES, LANES), 1))
    ri = jnp.bitwise_and(lax.broadcasted_iota(jnp.int32, (2 * c, 2 * c), 0), c - 1)
    ci = jnp.bitwise_and(lax.broadcasted_iota(jnp.int32, (2 * c, 2 * c), 1), c - 1)
    pair_level = jnp.where(ri >= ci, lax.clz(jnp.bitwise_xor(ri, ci)), -1)
    zero_q = jnp.zeros((c, LANES), F32)
    zero_v = jnp.zeros((c, GLA_DV), F32)

    units = [(s, ch) for s in range(n_seq) for ch in range(n_chunks)]
    rows = {u: slice(u[1] * c, (u[1] + 1) * c) for u in units}

    z = {u: _mm(hs_ref[u[0], rows[u], :], wup_ref[...]) + bg_ref[...] for u in units}
    la = {u: _log_sigmoid(z[u]) * (LOG2_E / GLA_GATE_TAU) for u in units}
    ex = {u: _sel_mm(cm_ref[...], la[u]) for u in units}
    b = {u: ex[u][0:c] for u in units}
    q = {u: hg_ref[u[0], rows[u], 0:GLA_QK] * (GLA_DK ** -0.5) for u in units}
    k = {u: hg_ref[u[0], rows[u], GLA_QK:2 * GLA_QK] for u in units}
    kg = {u: [k[u] * col_mask[g] for g in range(2)] for u in units}

    def level_weight(u, size, n_fine):
        if size // 2 < SUBLANES:
            d = ex[u][(1 + n_fine) * c:(2 + n_fine) * c]
        else:
            b_mid = jnp.concatenate([jnp.broadcast_to(b[u][m:m + 1, :], (size, GLA_QK))
                                     for m in range(size // 2, c, size)], axis=0)
            d = b[u] - b_mid
        return jnp.exp2(-jnp.abs(d))

    def stacked_scores(qw, kw0, kw1):
        lhs = jnp.concatenate([jnp.concatenate([qw[:, pair(0)], zero_q], axis=1),
                               jnp.concatenate([zero_q, qw[:, pair(1)]], axis=1)], axis=0)
        return _mm_nt(lhs, jnp.concatenate([kw0, kw1], axis=0))

    score = {u: jnp.zeros((2 * c, 2 * c), F32) for u in units}
    n_fine = 0
    for size in _gla_level_sizes(c):
        code = 31 - (size.bit_length() - 2)
        for u in units:
            w = level_weight(u, size, n_fine)
            a = stacked_scores(q[u] * w, kg[u][0] * w, kg[u][1] * w)
            score[u] = jnp.where(pair_level == code, a, score[u])
        if size // 2 < SUBLANES:
            n_fine += 1
    for u in units:
        a = stacked_scores(q[u], kg[u][0], kg[u][1])
        score[u] = jnp.where(pair_level == 32, a, score[u])

    v = {u: hg_ref[u[0], rows[u], 2 * GLA_QK:2 * GLA_QK + GLA_V] for u in units}
    o_intra = {}
    for u in units:
        for a in range(2):
            v_bd = jnp.concatenate(
                [jnp.concatenate([v[u][:, 2 * a * GLA_DV:(2 * a + 1) * GLA_DV], zero_v], axis=1),
                 jnp.concatenate([zero_v, v[u][:, (2 * a + 1) * GLA_DV:(2 * a + 2) * GLA_DV]], axis=1)], axis=0)
            o_pair = _mm(score[u][a * c:(a + 1) * c], v_bd)
            o_intra[u, 2 * a] = o_pair[:, 0:GLA_DV]
            o_intra[u, 2 * a + 1] = o_pair[:, GLA_DV:2 * GLA_DV]

    qhat = {u: q[u] * jnp.exp2(b[u]) for u in units}
    upd, decay = {}, {}
    for u in units:
        b_last = b[u][c - 1:c, :]
        ks = k[u] * jnp.exp2(b_last - b[u])
        for p in range(2):
            upd[u, p] = _mm_tn(ks[:, pair(p)], v[u][:, 2 * p * GLA_DV:(2 * p + 2) * GLA_DV]) * mask_bd
            bl = jnp.broadcast_to(b_last[:, pair(p)], (LANES, LANES))
            decay[u, p] = jnp.exp2(jnp.sum(jnp.where(eye128, bl, 0.0), axis=1, keepdims=True))

    for ch in range(n_chunks):
        for s in range(n_seq):
            u = (s, ch)
            for p in range(2):
                s_old = sbd_ref[s, p]
                inter = _mm(qhat[u][:, pair(p)], s_old)
                sbd_ref[s, p] = s_old * decay[u, p] + upd[u, p]
                for hh in range(2):
                    h = 2 * p + hh
                    hv = slice(h * GLA_DV, (h + 1) * GLA_DV)
                    o_ref[s, rows[u], hv] = inter[:, hh * GLA_DV:(hh + 1) * GLA_DV] + o_intra[u, h]

    @pl.when(t_idx == pl.num_programs(1) - 1)
    def _():
        for s in range(n_seq):
            for p in range(2):
                sout_ref[s, 2 * p] = sbd_ref[s, p, 0:GLA_DK, 0:GLA_DV]
                sout_ref[s, 2 * p + 1] = sbd_ref[s, p, GLA_DK:2 * GLA_DK, GLA_DV:2 * GLA_DV]


def _gla_call(h_gla, h_small, s0, wup_pad, b_gate, *, seqs_per_step, chunks_per_step):
    bsz, t_len, _ = h_gla.shape
    c = min(MAX_CHUNK, t_len)
    tb = c * chunks_per_step
    bb = seqs_per_step
    assert t_len % tb == 0 and bsz % bb == 0 and c >= SUBLANES and c & (c - 1) == 0
    cm = jnp.asarray(_gla_exponent_matrix(c), BF16)
    tok = lambda b, t: (b, t, 0)
    per_b = lambda b, t: (b, 0, 0, 0)
    kern = functools.partial(_gla_kernel, c=c, n_seq=bb, n_chunks=chunks_per_step)
    return pl.pallas_call(
        kern,
        grid=(bsz // bb, t_len // tb),
        in_specs=[pl.BlockSpec((bb, tb, 2 * GLA_QK + GLA_V), tok),
                  pl.BlockSpec((bb, tb, SMALL_COLS), tok),
                  pl.BlockSpec((bb, GLA_HEADS, GLA_DK, GLA_DV), per_b),
                  _const_spec(wup_pad.shape), _const_spec(b_gate.shape), _const_spec(cm.shape)],
        out_specs=[pl.BlockSpec((bb, tb, GLA_V), tok),
                   pl.BlockSpec((bb, GLA_HEADS, GLA_DK, GLA_DV), per_b)],
        out_shape=[jax.ShapeDtypeStruct((bsz, t_len, GLA_V), F32),
                   jax.ShapeDtypeStruct((bsz, GLA_HEADS, GLA_DK, GLA_DV), F32)],
        scratch_shapes=[pltpu.VMEM((bb, 2, 2 * GLA_DK, 2 * GLA_DV), F32)],
        compiler_params=pltpu.CompilerParams(dimension_semantics=("arbitrary", "arbitrary"),
                                             vmem_limit_bytes=VMEM_LIMIT),
        name="gla",
    )(h_gla, h_small, s0, wup_pad, b_gate, cm)


def _gdn_kernel(hd_ref, hs_ref, s0_ref, cb_ref, cw_ref, sel_ref, alog_ref, dtb_ref, tri_ref,
                o_ref, sout_ref, cbout_ref, xwin_ref, *, c, n_seq, n_chunks):
    t_idx = pl.program_id(1)
    tail = SUBLANES - (CONV_W - 1)
    tb = c * n_chunks

    @pl.when(t_idx == 0)
    def _():
        sout_ref[...] = s0_ref[...]
        for s in range(n_seq):
            xwin_ref[s, 0:SUBLANES, :] = jnp.zeros((SUBLANES, GDN_CONV_DIM), F32)
            xwin_ref[s, tail:SUBLANES, :] = cb_ref[s]

    ri = lax.broadcasted_iota(jnp.int32, (c, c), 0)
    ci = lax.broadcasted_iota(jnp.int32, (c, c), 1)
    eye = ri == ci
    causal = ri >= ci
    strict = ri > ci
    n_pow = c.bit_length() - 2
    units = [(s, ch) for s in range(n_seq) for ch in range(n_chunks)]
    heads = range(GDN_HEADS)
    uh = [(u, h) for u in units for h in heads]

    conv = {}
    for s in range(n_seq):
        xwin_ref[s, SUBLANES:SUBLANES + c, :] = hd_ref[s, 0:c, 0:GDN_CONV_DIM]
        for ch in range(n_chunks):
            r0 = ch * c
            acc = None
            for i in range(CONV_W):
                if ch == 0:
                    rows = xwin_ref[s, tail + i:tail + i + c, :]
                else:
                    rows = hd_ref[s, r0 - (CONV_W - 1) + i:r0 - (CONV_W - 1) + i + c, 0:GDN_CONV_DIM]
                term = rows * cw_ref[i:i + 1, :]
                acc = term if acc is None else acc + term
            conv[(s, ch)] = _silu(acc)
        xwin_ref[s, 0:SUBLANES, :] = hd_ref[s, tb - SUBLANES:tb, 0:GDN_CONV_DIM]

    lane_s = lax.broadcasted_iota(jnp.int32, (1, SMALL_COLS), 1)
    is_a = (lane_s >= SMALL_A0) & (lane_s < SMALL_A0 + GDN_HEADS)
    hs = {u: hs_ref[u[0], u[1] * c:(u[1] + 1) * c, :] for u in units}
    g_s = {u: -jnp.exp(alog_ref[...]) * _softplus(hs[u] + dtb_ref[...]) for u in units}
    gc_s = {u: _sel_mm(tri_ref[...], g_s[u]) for u in units}
    bc = {u: _mm_sel(jnp.where(is_a, gc_s[u], _sigmoid(hs[u])), sel_ref[...]) for u in units}
    gc = {u: bc[u][:, 0:GDN_V] for u in units}
    beta = {u: bc[u][:, GDN_V:2 * GDN_V] for u in units}

    q, k, gam, eg, bt = {}, {}, {}, {}, {}
    for (u, h) in uh:
        cv = conv[u]
        qh = cv[:, h * GDN_DK:(h + 1) * GDN_DK]
        kh = cv[:, GDN_QK + h * GDN_DK:GDN_QK + (h + 1) * GDN_DK]
        q[u, h] = qh * lax.rsqrt(jnp.sum(qh * qh, axis=-1, keepdims=True) + L2_EPS) * (GDN_DK ** -0.5)
        k[u, h] = kh * lax.rsqrt(jnp.sum(kh * kh, axis=-1, keepdims=True) + L2_EPS)
        gch = gc[u][:, h * LANES:(h + 1) * LANES]
        gcol = gch[:, 0:c]
        grow = jnp.sum(jnp.where(eye, gcol, 0.0), axis=0, keepdims=True)
        gam[u, h] = jnp.where(causal, jnp.exp(jnp.minimum(gcol - grow, 0.0)), 0.0)
        eg[u, h] = jnp.exp(gch)
        bt[u, h] = beta[u][:, h * LANES:(h + 1) * LANES]

    qkk = {x: _mm_nt(jnp.concatenate([q[x], k[x]], axis=0), k[x]) for x in uh}
    qk = {x: jnp.where(causal, qkk[x][0:c] * gam[x], 0.0) for x in uh}
    n_mat = {x: jnp.where(strict, bt[x][:, 0:c] * qkk[x][c:2 * c] * gam[x], 0.0) for x in uh}

    t_m = {x: -n_mat[x] for x in uh}
    x_pow = {x: _mm(n_mat[x], n_mat[x]) for x in uh}
    for _ in range(n_pow - 1):
        both = {x: _mm(jnp.concatenate([x_pow[x], t_m[x]], axis=0), x_pow[x]) for x in uh}
        t_m = {x: t_m[x] + x_pow[x] + both[x][c:2 * c] for x in uh}
        x_pow = {x: both[x][0:c] for x in uh}
    last = {x: _mm(t_m[x], x_pow[x]) for x in uh}
    t_m = {x: t_m[x] + x_pow[x] + last[x] for x in uh}

    rhs = {}
    for (u, h) in uh:
        vh = conv[u][:, 2 * GDN_QK + h * GDN_DV:2 * GDN_QK + (h + 1) * GDN_DV]
        rhs[u, h] = jnp.concatenate([bt[u, h] * vh, bt[u, h] * eg[u, h] * k[u, h]], axis=1)
    uw = {x: rhs[x] + _mm(t_m[x], rhs[x]) for x in uh}

    for ch in range(n_chunks):
        cur = [((s, ch), h) for s in range(n_seq) for h in heads]
        s_old = {x: sout_ref[x[0][0], x[1]] for x in cur}
        ws_qs = {x: _mm(jnp.concatenate([uw[x][:, GDN_DV:2 * GDN_DV], q[x] * eg[x]], axis=0), s_old[x])
                 for x in cur}
        delta = {x: uw[x][:, 0:GDN_DV] - ws_qs[x][0:c] for x in cur}
        for x in cur:
            (s, _), h = x
            hl = slice(h * LANES, (h + 1) * LANES)
            gch = gc[x[0]][:, hl]
            g_last = gch[c - 1:c, :]
            ks = k[x] * jnp.exp(g_last - gch)
            sout_ref[s, h] = s_old[x] * jnp.exp(g_last) + _mm_tn(ks, delta[x])
            o_ref[s, ch * c:(ch + 1) * c, hl] = ws_qs[x][c:2 * c] + _mm(qk[x], delta[x])

    @pl.when(t_idx == pl.num_programs(1) - 1)
    def _():
        for s in range(n_seq):
            cbout_ref[s] = xwin_ref[s, tail:SUBLANES, :]


def _gdn_call(h_gdn, h_small, s0, conv_buf, conv_w, a_log, dt_bias, *, seqs_per_step, chunks_per_step):
    bsz, t_len, _ = h_gdn.shape
    c = min(MAX_CHUNK, t_len)
    tb = c * chunks_per_step
    bb = seqs_per_step
    assert t_len % tb == 0 and bsz % bb == 0 and c >= SUBLANES and c & (c - 1) == 0
    sel = np.zeros((SMALL_COLS, 2 * GDN_V), np.float32)
    for h in range(GDN_HEADS):
        sel[SMALL_A0 + h, h * LANES:(h + 1) * LANES] = 1.0
        sel[SMALL_B0 + h, GDN_V + h * LANES:GDN_V + (h + 1) * LANES] = 1.0
    sel = jnp.asarray(sel, BF16)
    tri = jnp.asarray(np.tril(np.ones((c, c), np.float32)), BF16)
    alog_b = jnp.zeros((1, SMALL_COLS), F32).at[0, SMALL_A0:SMALL_A0 + GDN_HEADS].set(a_log.astype(F32))
    dtb_b = jnp.zeros((1, SMALL_COLS), F32).at[0, SMALL_A0:SMALL_A0 + GDN_HEADS].set(dt_bias.astype(F32))
    tok = lambda b, t: (b, t, 0)
    per_b4 = lambda b, t: (b, 0, 0, 0)
    per_b3 = lambda b, t: (b, 0, 0)
    kern = functools.partial(_gdn_kernel, c=c, n_seq=bb, n_chunks=chunks_per_step)
    return pl.pallas_call(
        kern,
        grid=(bsz // bb, t_len // tb),
        in_specs=[pl.BlockSpec((bb, tb, GDN_CONV_DIM), tok),
                  pl.BlockSpec((bb, tb, SMALL_COLS), tok),
                  pl.BlockSpec((bb, GDN_HEADS, GDN_DK, GDN_DV), per_b4),
                  pl.BlockSpec((bb, CONV_W - 1, GDN_CONV_DIM), per_b3),
                  _const_spec(conv_w.shape), _const_spec(sel.shape), _const_spec(alog_b.shape),
                  _const_spec(dtb_b.shape), _const_spec(tri.shape)],
        out_specs=[pl.BlockSpec((bb, tb, GDN_V), tok),
                   pl.BlockSpec((bb, GDN_HEADS, GDN_DK, GDN_DV), per_b4),
                   pl.BlockSpec((bb, CONV_W - 1, GDN_CONV_DIM), per_b3)],
        out_shape=[jax.ShapeDtypeStruct((bsz, t_len, GDN_V), F32),
                   jax.ShapeDtypeStruct((bsz, GDN_HEADS, GDN_DK, GDN_DV), F32),
                   jax.ShapeDtypeStruct((bsz, CONV_W - 1, GDN_CONV_DIM), F32)],
        scratch_shapes=[pltpu.VMEM((bb, SUBLANES + c, GDN_CONV_DIM), F32)],
        compiler_params=pltpu.CompilerParams(dimension_semantics=("arbitrary", "arbitrary"),
                                             vmem_limit_bytes=VMEM_LIMIT),
        name="gdn",
    )(h_gdn, h_small, s0, conv_buf, conv_w, sel, alog_b, dtb_b, tri)


def _out_ffn_kernel(x_ref, og_ref, od_ref, gg_ref, dg_ref, ngg_ref, ngd_ref, wo_ref, g1_ref, b1_ref,
                    wg_ref, wu_ref, wd_ref, g2_ref, b2_ref, y_ref, hid_ref, *, alpha, n_sub):
    tm = x_ref.shape[0]
    sub = [slice(i * (tm // n_sub), (i + 1) * (tm // n_sub)) for i in range(n_sub)]

    def gated(o_ref, gate_ref, ng_ref, r, n_heads, dv):
        return jnp.concatenate(
            [_rms_gate(o_ref[r, h * dv:(h + 1) * dv], ng_ref[...], gate_ref[r, h * dv:(h + 1) * dv])
             for h in range(n_heads)], axis=1).astype(BF16)

    o_g = [gated(og_ref, gg_ref, ngg_ref, r, GLA_HEADS, GLA_DV) for r in sub]
    o_d = [gated(od_ref, dg_ref, ngd_ref, r, GDN_HEADS, GDN_DV) for r in sub]
    m = [jnp.dot(o_g[i], wo_ref[0:GLA_V, :], preferred_element_type=F32)
         + jnp.dot(o_d[i], wo_ref[GLA_V:GLA_V + GDN_V, :], preferred_element_type=F32) for i in range(n_sub)]
    x1 = [_layer_norm(alpha * x_ref[r, :] + m[i], g1_ref[...], b1_ref[...]) for i, r in enumerate(sub)]
    x1b = [v.astype(BF16) for v in x1]
    for f0 in range(0, D_FF, FF_TILE):
        for i, r in enumerate(sub):
            gt = jnp.dot(x1b[i], wg_ref[:, f0:f0 + FF_TILE], preferred_element_type=F32)
            up = jnp.dot(x1b[i], wu_ref[:, f0:f0 + FF_TILE], preferred_element_type=F32)
            hid_ref[r, f0:f0 + FF_TILE] = (_silu(gt) * up).astype(BF16)
    f = [jnp.dot(hid_ref[r, :], wd_ref[...], preferred_element_type=F32) for r in sub]
    for i, r in enumerate(sub):
        y_ref[r, :] = _layer_norm(alpha * x1[i] + f[i], g2_ref[...], b2_ref[...])


def _out_ffn_call(x2d, o_gla, o_gdn, h_gla, h_gdn, gla_norm_g, gdn_norm_g, w_out, ln1_g, ln1_b,
                  w_gate, w_up, w_down, ln2_g, ln2_b, *, alpha):
    n_tok = x2d.shape[0]
    tm = 512 if n_tok % 512 == 0 else n_tok
    row = lambda i: (i, 0)
    assert (GLA_COLS - GLA_V) % GLA_V == 0 and (GDN_COLS - GDN_V) % GDN_V == 0
    gla_gate_blk = (GLA_COLS - GLA_V) // GLA_V
    gdn_gate_blk = (GDN_COLS - GDN_V) // GDN_V
    kern = functools.partial(_out_ffn_kernel, alpha=alpha, n_sub=2 if tm % 32 == 0 else 1)
    return pl.pallas_call(
        kern,
        grid=(n_tok // tm,),
        in_specs=[pl.BlockSpec((tm, D_MODEL), row), pl.BlockSpec((tm, GLA_V), row),
                  pl.BlockSpec((tm, GDN_V), row),
                  pl.BlockSpec((tm, GLA_V), lambda i: (i, gla_gate_blk)),
                  pl.BlockSpec((tm, GDN_V), lambda i: (i, gdn_gate_blk)),
                  _const_spec(gla_norm_g.shape), _const_spec(gdn_norm_g.shape),
                  _const_spec(w_out.shape), _const_spec(ln1_g.shape), _const_spec(ln1_b.shape),
                  _const_spec(w_gate.shape), _const_spec(w_up.shape), _const_spec(w_down.shape),
                  _const_spec(ln2_g.shape), _const_spec(ln2_b.shape)],
        out_specs=pl.BlockSpec((tm, D_MODEL), row),
        out_shape=jax.ShapeDtypeStruct((n_tok, D_MODEL), F32),
        scratch_shapes=[pltpu.VMEM((tm, D_FF), BF16)],
        compiler_params=pltpu.CompilerParams(dimension_semantics=("arbitrary",),
                                             vmem_limit_bytes=VMEM_LIMIT),
        name="out_ffn",
    )(x2d, o_gla, o_gdn, h_gla, h_gdn, gla_norm_g, gdn_norm_g, w_out, ln1_g, ln1_b,
      w_gate, w_up, w_down, ln2_g, ln2_b)


def _regroup_w_in(w_in):
    o = 0
    q0 = o; o += GLA_QK
    k0 = o; o += GLA_QK
    v0 = o; o += GLA_V
    gg0 = o; o += GLA_V
    ga0 = o; o += GLA_GATE_RANK
    dqkv0 = o; o += GDN_CONV_DIM
    dg0 = o; o += GDN_V
    da0 = o; o += GDN_HEADS
    db0 = o; o += GDN_HEADS
    assert w_in.shape[1] == o and (q0, k0, v0, gg0) == (0, GLA_QK, 2 * GLA_QK, 2 * GLA_QK + GLA_V)
    w_gla = w_in[:, 0:ga0]
    w_gdn = w_in[:, dqkv0:da0]
    pad = jnp.zeros((w_in.shape[0], SMALL_COLS - GLA_GATE_RANK - 2 * GDN_HEADS), w_in.dtype)
    w_small = jnp.concatenate([w_in[:, ga0:dqkv0], w_in[:, da0:o], pad], axis=1)
    return w_gla.astype(BF16), w_gdn.astype(BF16), w_small.astype(BF16)


def _trunk_layer(x, s_gla, s_gdn, conv_buf, p, *, alpha):
    bsz, t_len, _ = x.shape
    x2d = x.reshape(bsz * t_len, D_MODEL)
    h_gla2d, h_gdn2d, h_small = _inproj_call(x2d, p["w_gla"], p["w_gdn"], p["w_small"])
    h_gla = h_gla2d.reshape(bsz, t_len, GLA_COLS)
    h_gdn = h_gdn2d.reshape(bsz, t_len, GDN_COLS)
    h_small = h_small.reshape(bsz, t_len, SMALL_COLS)
    n_chunks = max(1, t_len // MAX_CHUNK)
    cps = 4 if n_chunks % 4 == 0 else 1
    sps = 1 if cps > 1 else (4 if bsz % 4 == 0 else 1)
    o_gla, s_gla_new = _gla_call(h_gla, h_small, s_gla, p["wup_pad"], p["b_gate"],
                                 seqs_per_step=sps, chunks_per_step=cps)
    o_gdn, s_gdn_new, buf_new = _gdn_call(h_gdn, h_small, s_gdn, conv_buf, p["conv_w"], p["a_log"],
                                          p["dt_bias"], seqs_per_step=sps, chunks_per_step=cps)
    y = _out_ffn_call(x2d, o_gla.reshape(bsz * t_len, GLA_V), o_gdn.reshape(bsz * t_len, GDN_V),
                      h_gla2d, h_gdn2d, p["gla_norm_g"], p["gdn_norm_g"],
                      p["w_out"], p["ln1_g"], p["ln1_b"], p["w_gate"], p["w_up"], p["w_down"],
                      p["ln2_g"], p["ln2_b"], alpha=alpha)
    return y.reshape(bsz, t_len, D_MODEL), s_gla_new, s_gdn_new, buf_new


def kernel(x_prompt, x_sample, state_gla, state_gdn, state_gdn_conv, w_in, gla_w_gate_up, gla_b_gate,
           gla_norm_g, gdn_conv_w, gdn_a_log, gdn_dt_bias, gdn_norm_g, w_out, ln1_g, ln1_b,
           w_ffn_gate, w_ffn_up, w_ffn_down, ln2_g, ln2_b):
    depth = w_in.shape[0]
    alpha = float((2 * depth) ** 0.25)
    bp = x_prompt.shape[0]
    yp, ys = x_prompt, x_sample
    outs = [[] for _ in range(6)]
    for l in range(depth):
        w_gla, w_gdn, w_small = _regroup_w_in(w_in[l])
        wup_pad = jnp.zeros((SMALL_COLS, GLA_QK), F32).at[0:GLA_GATE_RANK].set(
            gla_w_gate_up[l].astype(F32)).astype(BF16)
        p = dict(w_gla=w_gla, w_gdn=w_gdn, w_small=w_small, wup_pad=wup_pad,
                 b_gate=gla_b_gate[l].astype(F32)[None, :], gla_norm_g=gla_norm_g[l].astype(F32)[None, :],
                 conv_w=gdn_conv_w[l].astype(F32), a_log=gdn_a_log[l], dt_bias=gdn_dt_bias[l],
                 gdn_norm_g=gdn_norm_g[l].astype(F32)[None, :], w_out=w_out[l].astype(BF16),
                 ln1_g=ln1_g[l][None, :], ln1_b=ln1_b[l][None, :], w_gate=w_ffn_gate[l].astype(BF16),
                 w_up=w_ffn_up[l].astype(BF16), w_down=w_ffn_down[l].astype(BF16),
                 ln2_g=ln2_g[l][None, :], ln2_b=ln2_b[l][None, :])
        z_gla = jnp.zeros((bp,) + state_gla.shape[2:], state_gla.dtype)
        z_gdn = jnp.zeros((bp,) + state_gdn.shape[2:], state_gdn.dtype)
        z_conv = jnp.zeros((bp,) + state_gdn_conv.shape[2:], state_gdn_conv.dtype)
        yp, a1, a2, a3 = _trunk_layer(yp, z_gla, z_gdn, z_conv, p, alpha=alpha)
        ys, b1, b2, b3 = _trunk_layer(ys, state_gla[l], state_gdn[l], state_gdn_conv[l], p, alpha=alpha)
        for lst, val in zip(outs, (a1, a2, a3, b1, b2, b3)):
            lst.append(val)
    return (yp, ys) + tuple(jnp.stack(o) for o in outs)
```

```python
import functools

import numpy as np
import jax
import jax.numpy as jnp
from jax import lax
from jax.experimental import pallas as pl
from jax.experimental.pallas import tpu as pltpu

F32 = jnp.float32
BF16 = jnp.bfloat16

D_MODEL = 1024
GLA_HEADS, GLA_DK, GLA_DV = 4, 64, 128
GLA_QK = GLA_HEADS * GLA_DK
GLA_V = GLA_HEADS * GLA_DV
GLA_GATE_RANK = 16
GLA_GATE_TAU = 16.0
GDN_HEADS, GDN_DK, GDN_DV = 4, 128, 128
GDN_QK = GDN_HEADS * GDN_DK
GDN_V = GDN_HEADS * GDN_DV
GDN_CONV_DIM = 2 * GDN_QK + GDN_V
CONV_W = 4
D_FF = 2816
NORM_EPS = 1e-5
L2_EPS = 1e-6

GLA_COLS = 2 * GLA_QK + 2 * GLA_V
GDN_COLS = GDN_CONV_DIM + GDN_V
SMALL_COLS = 128
SMALL_A0 = GLA_GATE_RANK
SMALL_B0 = GLA_GATE_RANK + GDN_HEADS

MAX_CHUNK = 64
LOG2_E = 1.4426950408889634
LANES = 128
SUBLANES = 8
VMEM_LIMIT = 56 * 1024 * 1024
FF_TILE = 256
INPROJ_COLS = 256


def _mm(a, b):
    return jnp.dot(a.astype(BF16), b.astype(BF16), preferred_element_type=F32)


def _mm_nt(a, b):
    return lax.dot_general(a.astype(BF16), b.astype(BF16), (((1,), (1,)), ((), ())),
                           preferred_element_type=F32)


def _mm_tn(a, b):
    return lax.dot_general(a.astype(BF16), b.astype(BF16), (((0,), (0,)), ((), ())),
                           preferred_element_type=F32)


def _split3(x):
    hi = x.astype(BF16)
    r = x - hi.astype(F32)
    mid = r.astype(BF16)
    lo = (r - mid.astype(F32)).astype(BF16)
    return hi, mid, lo


def _sel_mm(c, x):
    hi, mid, lo = _split3(x)
    d = functools.partial(jnp.dot, preferred_element_type=F32)
    return (d(c, lo) + d(c, mid)) + d(c, hi)


def _mm_sel(x, c):
    hi, mid, lo = _split3(x)
    d = functools.partial(jnp.dot, preferred_element_type=F32)
    return (d(lo, c) + d(mid, c)) + d(hi, c)


def _sigmoid(x):
    return 1.0 / (1.0 + jnp.exp(-x))


def _silu(x):
    return x * _sigmoid(x)


def _softplus(x):
    return jnp.maximum(x, 0.0) + jnp.log1p(jnp.exp(-jnp.abs(x)))


def _log_sigmoid(x):
    return jnp.minimum(x, 0.0) - jnp.log1p(jnp.exp(-jnp.abs(x)))


def _layer_norm(x, g, b):
    mu = jnp.mean(x, axis=-1, keepdims=True)
    xc = x - mu
    var = jnp.mean(xc * xc, axis=-1, keepdims=True)
    return xc * lax.rsqrt(var + NORM_EPS) * g + b


def _rms_gate(o, g, gate):
    ms = jnp.mean(o * o, axis=-1, keepdims=True)
    return o * lax.rsqrt(ms + NORM_EPS) * g * _silu(gate)


def _const_spec(shape):
    nd = len(shape)
    return pl.BlockSpec(shape, lambda *_: (0,) * nd, pipeline_mode=pl.Buffered(1))


def _inproj_kernel(x_ref, wg_ref, wd_ref, ws_ref, cb_ref, cw_ref, og_ref, od_ref, os_ref, cbout_ref,
                   *xwin_refs):
    bb, tb, _ = x_ref.shape
    tm = bb * tb
    t_idx = pl.program_id(1)
    tail = SUBLANES - (CONV_W - 1)
    x = x_ref[...].reshape(tm, D_MODEL).astype(BF16)
    step = INPROJ_COLS
    xwin = {c0: xwin_refs[j] for j, c0 in enumerate(range(0, GDN_CONV_DIM, step))}

    @pl.when(t_idx == 0)
    def _():
        for c0, ref in xwin.items():
            ref[:, 0:SUBLANES, :] = jnp.zeros((bb, SUBLANES, step), F32)
            ref[:, tail:SUBLANES, :] = cb_ref[:, :, c0:c0 + step]

    def proj_conv_cols(c0):
        h = jnp.dot(x, wd_ref[:, c0:c0 + step], preferred_element_type=F32)
        xwin[c0][:, SUBLANES:SUBLANES + tb, :] = h.reshape(bb, tb, step)

    def conv_chunk(c0):
        win = xwin[c0][...]
        acc = None
        for i in range(CONV_W):
            shift = CONV_W - 1 - i
            rows = win if shift == 0 else pltpu.roll(win, shift, axis=1)
            term = rows[:, SUBLANES:SUBLANES + tb, :] * cw_ref[i:i + 1, c0:c0 + step]
            acc = term if acc is None else acc + term
        conv = _silu(acc).reshape(tm, step)
        for h0 in range(0, step, GDN_DK):
            col = c0 + h0
            part = conv[:, h0:h0 + GDN_DK]
            if col < 2 * GDN_QK:
                scale = GDN_DK ** -0.5 if col < GDN_QK else 1.0
                part = part * (lax.rsqrt(jnp.sum(part * part, axis=-1, keepdims=True) + L2_EPS) * scale)
            od_ref[:, col:col + GDN_DK] = part

    def plain_cols(w_ref, o_ref, c0, c1):
        o_ref[:, c0:c1] = jnp.dot(x, w_ref[:, c0:c1], preferred_element_type=F32)

    conv_cols = list(range(0, GDN_CONV_DIM, step))
    mm_jobs = [functools.partial(proj_conv_cols, c0) for c0 in conv_cols]
    mm_jobs += [functools.partial(plain_cols, wd_ref, od_ref, c0, c0 + step)
                for c0 in range(GDN_CONV_DIM, GDN_COLS, step)]
    mm_jobs += [functools.partial(plain_cols, wg_ref, og_ref, c0, c0 + step) for c0 in range(0, GLA_COLS, step)]
    mm_jobs.append(functools.partial(plain_cols, ws_ref, os_ref, 0, SMALL_COLS))
    n_conv_done = 0
    for j, job in enumerate(mm_jobs):
        job()
        if j % 2 == 1 and n_conv_done < len(conv_cols):
            conv_chunk(conv_cols[n_conv_done])
            n_conv_done += 1
    for c0 in conv_cols[n_conv_done:]:
        conv_chunk(c0)
    for ref in xwin.values():
        ref[:, 0:SUBLANES, :] = ref[:, tb:tb + SUBLANES, :]

    @pl.when(t_idx == pl.num_programs(1) - 1)
    def _():
        for c0, ref in xwin.items():
            cbout_ref[:, :, c0:c0 + step] = ref[:, tail:SUBLANES, :]


def _inproj_call(x, conv_buf, w_gla, w_gdn, w_small, conv_w):
    bsz, t_len, _ = x.shape
    tb = min(t_len, 512)
    bb = max(1, min(bsz, 512 // tb))
    assert t_len % tb == 0 and bsz % bb == 0 and tb % SUBLANES == 0
    tm = bb * tb
    nt = t_len // tb
    n_tok = bsz * t_len
    row = lambda b, t: (b * nt + t, 0)
    per_b = lambda b, t: (b, 0, 0)
    return pl.pallas_call(
        _inproj_kernel,
        grid=(bsz // bb, nt),
        in_specs=[pl.BlockSpec((bb, tb, D_MODEL), lambda b, t: (b, t, 0)),
                  _const_spec(w_gla.shape), _const_spec(w_gdn.shape), _const_spec(w_small.shape),
                  pl.BlockSpec((bb, CONV_W - 1, GDN_CONV_DIM), per_b), _const_spec(conv_w.shape)],
        out_specs=[pl.BlockSpec((tm, GLA_COLS), row), pl.BlockSpec((tm, GDN_COLS), row),
                   pl.BlockSpec((tm, SMALL_COLS), row),
                   pl.BlockSpec((bb, CONV_W - 1, GDN_CONV_DIM), per_b)],
        out_shape=[jax.ShapeDtypeStruct((n_tok, GLA_COLS), F32),
                   jax.ShapeDtypeStruct((n_tok, GDN_COLS), F32),
                   jax.ShapeDtypeStruct((n_tok, SMALL_COLS), F32),
                   jax.ShapeDtypeStruct((bsz, CONV_W - 1, GDN_CONV_DIM), F32)],
        scratch_shapes=[pltpu.VMEM((bb, SUBLANES + tb, INPROJ_COLS), F32)
                        for _ in range(GDN_CONV_DIM // INPROJ_COLS)],
        compiler_params=pltpu.CompilerParams(dimension_semantics=("arbitrary", "arbitrary"),
                                             vmem_limit_bytes=VMEM_LIMIT),
        name="inproj",
    )(x, w_gla, w_gdn, w_small, conv_buf, conv_w)


def _gla_exponent_matrix(c):
    i = np.arange(c)[:, None]
    t = np.arange(c)[None, :]
    mats = [(t <= i).astype(np.float32)]
    for size in _gla_level_sizes(c):
        if size // 2 < SUBLANES:
            m = (i // size) * size + size // 2
            mats.append(((t > m) & (t <= i)).astype(np.float32) - ((t > i) & (t <= m)).astype(np.float32))
    return np.concatenate(mats, axis=0)


def _gla_level_sizes(c):
    sizes, size = [], c
    while size >= 2:
        sizes.append(size)
        size //= 2
    return sizes


def _gla_kernel(hg_ref, hs_ref, s0_ref, wup_ref, bg_ref, cm_ref,
                o_ref, sout_ref, sbd_ref, *, c, n_seq, n_chunks):
    t_idx = pl.program_id(1)
    pair = lambda p: slice(p * LANES, (p + 1) * LANES)

    @pl.when(t_idx == 0)
    def _():
        sbd_ref[...] = jnp.zeros_like(sbd_ref)
        for s in range(n_seq):
            for p in range(2):
                sbd_ref[s, p, 0:GLA_DK, 0:GLA_DV] = s0_ref[s, 2 * p]
                sbd_ref[s, p, GLA_DK:2 * GLA_DK, GLA_DV:2 * GLA_DV] = s0_ref[s, 2 * p + 1]

    lane = lax.broadcasted_iota(jnp.int32, (1, GLA_QK), 1)
    col_mask = [(jnp.bitwise_and(lax.shift_right_logical(lane, 6), 1) == g).astype(F32) for g in range(2)]
    r_bd = lax.shift_right_logical(lax.broadcasted_iota(jnp.int32, (2 * GLA_DK, 2 * GLA_DV), 0), 6)
    c_bd = lax.shift_right_logical(lax.broadcasted_iota(jnp.int32, (2 * GLA_DK, 2 * GLA_DV), 1), 7)
    mask_bd = (r_bd == c_bd).astype(F32)
    eye128 = (lax.broadcasted_iota(jnp.int32, (LANES, LANES), 0)
              == lax.broadcasted_iota(jnp.int32, (LANES, LANES), 1))
    ri = jnp.bitwise_and(lax.broadcasted_iota(jnp.int32, (2 * c, 2 * c), 0), c - 1)
    ci = jnp.bitwise_and(lax.broadcasted_iota(jnp.int32, (2 * c, 2 * c), 1), c - 1)
    pair_level = jnp.where(ri >= ci, lax.clz(jnp.bitwise_xor(ri, ci)), -1)
    zero_q = jnp.zeros((c, LANES), F32)
    zero_v = jnp.zeros((c, GLA_DV), F32)

    units = [(s, ch) for s in range(n_seq) for ch in range(n_chunks)]
    rows = {u: slice(u[1] * c, (u[1] + 1) * c) for u in units}

    z = {u: _mm(hs_ref[u[0], rows[u], :], wup_ref[...]) + bg_ref[...] for u in units}
    la = {u: _log_sigmoid(z[u]) * (LOG2_E / GLA_GATE_TAU) for u in units}
    ex = {u: _sel_mm(cm_ref[...], la[u]) for u in units}
    b = {u: ex[u][0:c] for u in units}
    q = {u: hg_ref[u[0], rows[u], 0:GLA_QK] * (GLA_DK ** -0.5) for u in units}
    k = {u: hg_ref[u[0], rows[u], GLA_QK:2 * GLA_QK] for u in units}
    kg = {u: [k[u] * col_mask[g] for g in range(2)] for u in units}

    def level_weight(u, size, n_fine):
        if size // 2 < SUBLANES:
            d = ex[u][(1 + n_fine) * c:(2 + n_fine) * c]
        else:
            b_mid = jnp.concatenate([jnp.broadcast_to(b[u][m:m + 1, :], (size, GLA_QK))
                                     for m in range(size // 2, c, size)], axis=0)
            d = b[u] - b_mid
        return jnp.exp2(-jnp.abs(d))

    def stacked_scores(qw, kw0, kw1):
        lhs = jnp.concatenate([jnp.concatenate([qw[:, pair(0)], zero_q], axis=1),
                               jnp.concatenate([zero_q, qw[:, pair(1)]], axis=1)], axis=0)
        return _mm_nt(lhs, jnp.concatenate([kw0, kw1], axis=0))

    score = {u: jnp.zeros((2 * c, 2 * c), F32) for u in units}
    n_fine = 0
    for size in _gla_level_sizes(c):
        code = 31 - (size.bit_length() - 2)
        for u in units:
            w = level_weight(u, size, n_fine)
            a = stacked_scores(q[u] * w, kg[u][0] * w, kg[u][1] * w)
            score[u] = jnp.where(pair_level == code, a, score[u])
        if size // 2 < SUBLANES:
            n_fine += 1
    for u in units:
        a = stacked_scores(q[u], kg[u][0], kg[u][1])
        score[u] = jnp.where(pair_level == 32, a, score[u])

    v = {u: hg_ref[u[0], rows[u], 2 * GLA_QK:2 * GLA_QK + GLA_V] for u in units}
    o_intra = {}
    for u in units:
        for a in range(2):
            v_bd = jnp.concatenate(
                [jnp.concatenate([v[u][:, 2 * a * GLA_DV:(2 * a + 1) * GLA_DV], zero_v], axis=1),
                 jnp.concatenate([zero_v, v[u][:, (2 * a + 1) * GLA_DV:(2 * a + 2) * GLA_DV]], axis=1)], axis=0)
            o_pair = _mm(score[u][a * c:(a + 1) * c], v_bd)
            o_intra[u, 2 * a] = o_pair[:, 0:GLA_DV]
            o_intra[u, 2 * a + 1] = o_pair[:, GLA_DV:2 * GLA_DV]

    qhat = {u: q[u] * jnp.exp2(b[u]) for u in units}
    upd, decay = {}, {}
    for u in units:
        b_last = b[u][c - 1:c, :]
        ks = k[u] * jnp.exp2(b_last - b[u])
        for p in range(2):
            upd[u, p] = _mm_tn(ks[:, pair(p)], v[u][:, 2 * p * GLA_DV:(2 * p + 2) * GLA_DV]) * mask_bd
            bl = jnp.broadcast_to(b_last[:, pair(p)], (LANES, LANES))
            decay[u, p] = jnp.exp2(jnp.sum(jnp.where(eye128, bl, 0.0), axis=1, keepdims=True))

    for ch in range(n_chunks):
        for s in range(n_seq):
            u = (s, ch)
            for p in range(2):
                s_old = sbd_ref[s, p]
                inter = _mm(qhat[u][:, pair(p)], s_old)
                sbd_ref[s, p] = s_old * decay[u, p] + upd[u, p]
                for hh in range(2):
                    h = 2 * p + hh
                    hv = slice(h * GLA_DV, (h + 1) * GLA_DV)
                    o_ref[s, rows[u], hv] = inter[:, hh * GLA_DV:(hh + 1) * GLA_DV] + o_intra[u, h]

    @pl.when(t_idx == pl.num_programs(1) - 1)
    def _():
        for s in range(n_seq):
            for p in range(2):
                sout_ref[s, 2 * p] = sbd_ref[s, p, 0:GLA_DK, 0:GLA_DV]
                sout_ref[s, 2 * p + 1] = sbd_ref[s, p, GLA_DK:2 * GLA_DK, GLA_DV:2 * GLA_DV]


def _gla_call(h_gla, h_small, s0, wup_pad, b_gate, *, seqs_per_step, chunks_per_step):
    bsz, t_len, _ = h_gla.shape
    c = min(MAX_CHUNK, t_len)
    tb = c * chunks_per_step
    bb = seqs_per_step
    assert t_len % tb == 0 and bsz % bb == 0 and c >= SUBLANES and c & (c - 1) == 0
    cm = jnp.asarray(_gla_exponent_matrix(c), BF16)
    tok = lambda b, t: (b, t, 0)
    per_b = lambda b, t: (b, 0, 0, 0)
    kern = functools.partial(_gla_kernel, c=c, n_seq=bb, n_chunks=chunks_per_step)
    return pl.pallas_call(
        kern,
        grid=(bsz // bb, t_len // tb),
        in_specs=[pl.BlockSpec((bb, tb, 2 * GLA_QK + GLA_V), tok),
                  pl.BlockSpec((bb, tb, SMALL_COLS), tok),
                  pl.BlockSpec((bb, GLA_HEADS, GLA_DK, GLA_DV), per_b),
                  _const_spec(wup_pad.shape), _const_spec(b_gate.shape), _const_spec(cm.shape)],
        out_specs=[pl.BlockSpec((bb, tb, GLA_V), tok),
                   pl.BlockSpec((bb, GLA_HEADS, GLA_DK, GLA_DV), per_b)],
        out_shape=[jax.ShapeDtypeStruct((bsz, t_len, GLA_V), F32),
                   jax.ShapeDtypeStruct((bsz, GLA_HEADS, GLA_DK, GLA_DV), F32)],
        scratch_shapes=[pltpu.VMEM((bb, 2, 2 * GLA_DK, 2 * GLA_DV), F32)],
        compiler_params=pltpu.CompilerParams(dimension_semantics=("arbitrary", "arbitrary"),
                                             vmem_limit_bytes=VMEM_LIMIT),
        name="gla",
    )(h_gla, h_small, s0, wup_pad, b_gate, cm)


def _gdn_kernel(hd_ref, hs_ref, s0_ref, sel_ref, alog_ref, dtb_ref, tri_ref,
                o_ref, sout_ref, *, c, n_seq, n_chunks):
    t_idx = pl.program_id(1)

    @pl.when(t_idx == 0)
    def _():
        sout_ref[...] = s0_ref[...]

    ri = lax.broadcasted_iota(jnp.int32, (c, c), 0)
    ci = lax.broadcasted_iota(jnp.int32, (c, c), 1)
    eye = ri == ci
    causal = ri >= ci
    strict = ri > ci
    n_pow = c.bit_length() - 2
    units = [(s, ch) for s in range(n_seq) for ch in range(n_chunks)]
    heads = range(GDN_HEADS)
    uh = [(u, h) for u in units for h in heads]

    rows = {u: slice(u[1] * c, (u[1] + 1) * c) for u in units}

    lane_s = lax.broadcasted_iota(jnp.int32, (1, SMALL_COLS), 1)
    is_a = (lane_s >= SMALL_A0) & (lane_s < SMALL_A0 + GDN_HEADS)
    hs = {u: hs_ref[u[0], u[1] * c:(u[1] + 1) * c, :] for u in units}
    g_s = {u: -jnp.exp(alog_ref[...]) * _softplus(hs[u] + dtb_ref[...]) for u in units}
    gc_s = {u: _sel_mm(tri_ref[...], g_s[u]) for u in units}
    bc = {u: _mm_sel(jnp.where(is_a, gc_s[u], _sigmoid(hs[u])), sel_ref[...]) for u in units}
    gc = {u: bc[u][:, 0:GDN_V] for u in units}
    beta = {u: bc[u][:, GDN_V:2 * GDN_V] for u in units}

    q, k, gam, eg, bt = {}, {}, {}, {}, {}
    for (u, h) in uh:
        q[u, h] = hd_ref[u[0], rows[u], h * GDN_DK:(h + 1) * GDN_DK]
        k[u, h] = hd_ref[u[0], rows[u], GDN_QK + h * GDN_DK:GDN_QK + (h + 1) * GDN_DK]
        gch = gc[u][:, h * LANES:(h + 1) * LANES]
        gcol = gch[:, 0:c]
        grow = jnp.sum(jnp.where(eye, gcol, 0.0), axis=0, keepdims=True)
        gam[u, h] = jnp.where(causal, jnp.exp(jnp.minimum(gcol - grow, 0.0)), 0.0)
        eg[u, h] = jnp.exp(gch)
        bt[u, h] = beta[u][:, h * LANES:(h + 1) * LANES]

    qkk = {x: _mm_nt(jnp.concatenate([q[x], k[x]], axis=0), k[x]) for x in uh}
    qk = {x: jnp.where(causal, qkk[x][0:c] * gam[x], 0.0) for x in uh}
    n_mat = {x: jnp.where(strict, bt[x][:, 0:c] * qkk[x][c:2 * c] * gam[x], 0.0) for x in uh}

    t_m = {x: -n_mat[x] for x in uh}
    x_pow = {x: _mm(n_mat[x], n_mat[x]) for x in uh}
    for _ in range(n_pow - 1):
        both = {x: _mm(jnp.concatenate([x_pow[x], t_m[x]], axis=0), x_pow[x]) for x in uh}
        t_m = {x: t_m[x] + x_pow[x] + both[x][c:2 * c] for x in uh}
        x_pow = {x: both[x][0:c] for x in uh}
    last = {x: _mm(t_m[x], x_pow[x]) for x in uh}
    t_m = {x: t_m[x] + x_pow[x] + last[x] for x in uh}

    rhs = {}
    for (u, h) in uh:
        vh = hd_ref[u[0], rows[u], 2 * GDN_QK + h * GDN_DV:2 * GDN_QK + (h + 1) * GDN_DV]
        rhs[u, h] = jnp.concatenate([bt[u, h] * vh, bt[u, h] * eg[u, h] * k[u, h]], axis=1)
    uw = {x: rhs[x] + _mm(t_m[x], rhs[x]) for x in uh}

    for ch in range(n_chunks):
        cur = [((s, ch), h) for s in range(n_seq) for h in heads]
        s_old = {x: sout_ref[x[0][0], x[1]] for x in cur}
        ws_qs = {x: _mm(jnp.concatenate([uw[x][:, GDN_DV:2 * GDN_DV], q[x] * eg[x]], axis=0), s_old[x])
                 for x in cur}
        delta = {x: uw[x][:, 0:GDN_DV] - ws_qs[x][0:c] for x in cur}
        for x in cur:
            (s, _), h = x
            hl = slice(h * LANES, (h + 1) * LANES)
            gch = gc[x[0]][:, hl]
            g_last = gch[c - 1:c, :]
            ks = k[x] * jnp.exp(g_last - gch)
            sout_ref[s, h] = s_old[x] * jnp.exp(g_last) + _mm_tn(ks, delta[x])
            o_ref[s, ch * c:(ch + 1) * c, hl] = ws_qs[x][c:2 * c] + _mm(qk[x], delta[x])


def _gdn_call(h_gdn, h_small, s0, a_log, dt_bias, *, seqs_per_step, chunks_per_step):
    bsz, t_len, _ = h_gdn.shape
    c = min(MAX_CHUNK, t_len)
    tb = c * chunks_per_step
    bb = seqs_per_step
    assert t_len % tb == 0 and bsz % bb == 0 and c >= SUBLANES and c & (c - 1) == 0
    sel = np.zeros((SMALL_COLS, 2 * GDN_V), np.float32)
    for h in range(GDN_HEADS):
        sel[SMALL_A0 + h, h * LANES:(h + 1) * LANES] = 1.0
        sel[SMALL_B0 + h, GDN_V + h * LANES:GDN_V + (h + 1) * LANES] = 1.0
    sel = jnp.asarray(sel, BF16)
    tri = jnp.asarray(np.tril(np.ones((c, c), np.float32)), BF16)
    alog_b = jnp.zeros((1, SMALL_COLS), F32).at[0, SMALL_A0:SMALL_A0 + GDN_HEADS].set(a_log.astype(F32))
    dtb_b = jnp.zeros((1, SMALL_COLS), F32).at[0, SMALL_A0:SMALL_A0 + GDN_HEADS].set(dt_bias.astype(F32))
    tok = lambda b, t: (b, t, 0)
    per_b4 = lambda b, t: (b, 0, 0, 0)
    kern = functools.partial(_gdn_kernel, c=c, n_seq=bb, n_chunks=chunks_per_step)
    return pl.pallas_call(
        kern,
        grid=(bsz // bb, t_len // tb),
        in_specs=[pl.BlockSpec((bb, tb, GDN_CONV_DIM), tok),
                  pl.BlockSpec((bb, tb, SMALL_COLS), tok),
                  pl.BlockSpec((bb, GDN_HEADS, GDN_DK, GDN_DV), per_b4),
                  _const_spec(sel.shape), _const_spec(alog_b.shape),
                  _const_spec(dtb_b.shape), _const_spec(tri.shape)],
        out_specs=[pl.BlockSpec((bb, tb, GDN_V), tok),
                   pl.BlockSpec((bb, GDN_HEADS, GDN_DK, GDN_DV), per_b4)],
        out_shape=[jax.ShapeDtypeStruct((bsz, t_len, GDN_V), F32),
                   jax.ShapeDtypeStruct((bsz, GDN_HEADS, GDN_DK, GDN_DV), F32)],
        compiler_params=pltpu.CompilerParams(dimension_semantics=("arbitrary", "arbitrary"),
                                             vmem_limit_bytes=VMEM_LIMIT),
        name="gdn",
    )(h_gdn, h_small, s0, sel, alog_b, dtb_b, tri)


def _out_ffn_kernel(x_ref, og_ref, od_ref, gg_ref, dg_ref, ngg_ref, ngd_ref, wo_ref, g1_ref, b1_ref,
                    wg_ref, wu_ref, wd_ref, g2_ref, b2_ref, y_ref, hid_ref, *, alpha, n_sub):
    tm = x_ref.shape[0]
    sub = [slice(i * (tm // n_sub), (i + 1) * (tm // n_sub)) for i in range(n_sub)]

    def gated(o_ref, gate_ref, ng_ref, r, n_heads, dv):
        return jnp.concatenate(
            [_rms_gate(o_ref[r, h * dv:(h + 1) * dv], ng_ref[...], gate_ref[r, h * dv:(h + 1) * dv])
             for h in range(n_heads)], axis=1).astype(BF16)

    o_g = [gated(og_ref, gg_ref, ngg_ref, r, GLA_HEADS, GLA_DV) for r in sub]
    o_d = [gated(od_ref, dg_ref, ngd_ref, r, GDN_HEADS, GDN_DV) for r in sub]
    m = [jnp.dot(o_g[i], wo_ref[0:GLA_V, :], preferred_element_type=F32)
         + jnp.dot(o_d[i], wo_ref[GLA_V:GLA_V + GDN_V, :], preferred_element_type=F32) for i in range(n_sub)]
    x1 = [_layer_norm(alpha * x_ref[r, :] + m[i], g1_ref[...], b1_ref[...]) for i, r in enumerate(sub)]
    x1b = [v.astype(BF16) for v in x1]
    for f0 in range(0, D_FF, FF_TILE):
        for i, r in enumerate(sub):
            gt = jnp.dot(x1b[i], wg_ref[:, f0:f0 + FF_TILE], preferred_element_type=F32)
            up = jnp.dot(x1b[i], wu_ref[:, f0:f0 + FF_TILE], preferred_element_type=F32)
            hid_ref[r, f0:f0 + FF_TILE] = (_silu(gt) * up).astype(BF16)
    f = [jnp.dot(hid_ref[r, :], wd_ref[...], preferred_element_type=F32) for r in sub]
    for i, r in enumerate(sub):
        y_ref[r, :] = _layer_norm(alpha * x1[i] + f[i], g2_ref[...], b2_ref[...])


def _out_ffn_call(x2d, o_gla, o_gdn, h_gla, h_gdn, gla_norm_g, gdn_norm_g, w_out, ln1_g, ln1_b,
                  w_gate, w_up, w_down, ln2_g, ln2_b, *, alpha):
    n_tok = x2d.shape[0]
    tm = 512 if n_tok % 512 == 0 else n_tok
    row = lambda i: (i, 0)
    assert (GLA_COLS - GLA_V) % GLA_V == 0 and (GDN_COLS - GDN_V) % GDN_V == 0
    gla_gate_blk = (GLA_COLS - GLA_V) // GLA_V
    gdn_gate_blk = (GDN_COLS - GDN_V) // GDN_V
    kern = functools.partial(_out_ffn_kernel, alpha=alpha, n_sub=2 if tm % 32 == 0 else 1)
    return pl.pallas_call(
        kern,
        grid=(n_tok // tm,),
        in_specs=[pl.BlockSpec((tm, D_MODEL), row), pl.BlockSpec((tm, GLA_V), row),
                  pl.BlockSpec((tm, GDN_V), row),
                  pl.BlockSpec((tm, GLA_V), lambda i: (i, gla_gate_blk)),
                  pl.BlockSpec((tm, GDN_V), lambda i: (i, gdn_gate_blk)),
                  _const_spec(gla_norm_g.shape), _const_spec(gdn_norm_g.shape),
                  _const_spec(w_out.shape), _const_spec(ln1_g.shape), _const_spec(ln1_b.shape),
                  _const_spec(w_gate.shape), _const_spec(w_up.shape), _const_spec(w_down.shape),
                  _const_spec(ln2_g.shape), _const_spec(ln2_b.shape)],
        out_specs=pl.BlockSpec((tm, D_MODEL), row),
        out_shape=jax.ShapeDtypeStruct((n_tok, D_MODEL), F32),
        scratch_shapes=[pltpu.VMEM((tm, D_FF), BF16)],
        compiler_params=pltpu.CompilerParams(dimension_semantics=("arbitrary",),
                                             vmem_limit_bytes=VMEM_LIMIT),
        name="out_ffn",
    )(x2d, o_gla, o_gdn, h_gla, h_gdn, gla_norm_g, gdn_norm_g, w_out, ln1_g, ln1_b,
      w_gate, w_up, w_down, ln2_g, ln2_b)


def _regroup_w_in(w_in):
    o = 0
    q0 = o; o += GLA_QK
    k0 = o; o += GLA_QK
    v0 = o; o += GLA_V
    gg0 = o; o += GLA_V
    ga0 = o; o += GLA_GATE_RANK
    dqkv0 = o; o += GDN_CONV_DIM
    dg0 = o; o += GDN_V
    da0 = o; o += GDN_HEADS
    db0 = o; o += GDN_HEADS
    assert w_in.shape[1] == o and (q0, k0, v0, gg0) == (0, GLA_QK, 2 * GLA_QK, 2 * GLA_QK + GLA_V)
    w_gla = w_in[:, 0:ga0]
    w_gdn = w_in[:, dqkv0:da0]
    pad = jnp.zeros((w_in.shape[0], SMALL_COLS - GLA_GATE_RANK - 2 * GDN_HEADS), w_in.dtype)
    w_small = jnp.concatenate([w_in[:, ga0:dqkv0], w_in[:, da0:o], pad], axis=1)
    return w_gla.astype(BF16), w_gdn.astype(BF16), w_small.astype(BF16)


def _trunk_layer(x, s_gla, s_gdn, conv_buf, p, *, alpha):
    bsz, t_len, _ = x.shape
    x2d = x.reshape(bsz * t_len, D_MODEL)
    h_gla2d, h_gdn2d, h_small, buf_new = _inproj_call(x, conv_buf, p["w_gla"], p["w_gdn"], p["w_small"],
                                                      p["conv_w"])
    h_gla = h_gla2d.reshape(bsz, t_len, GLA_COLS)
    h_gdn = h_gdn2d.reshape(bsz, t_len, GDN_COLS)
    h_small = h_small.reshape(bsz, t_len, SMALL_COLS)
    n_chunks = max(1, t_len // MAX_CHUNK)
    cps = 4 if n_chunks % 4 == 0 else 1
    sps = 1 if cps > 1 else (4 if bsz % 4 == 0 else 1)
    o_gla, s_gla_new = _gla_call(h_gla, h_small, s_gla, p["wup_pad"], p["b_gate"],
                                 seqs_per_step=sps, chunks_per_step=cps)
    o_gdn, s_gdn_new = _gdn_call(h_gdn, h_small, s_gdn, p["a_log"], p["dt_bias"],
                                 seqs_per_step=sps, chunks_per_step=cps)
    y = _out_ffn_call(x2d, o_gla.reshape(bsz * t_len, GLA_V), o_gdn.reshape(bsz * t_len, GDN_V),
                      h_gla2d, h_gdn2d, p["gla_norm_g"], p["gdn_norm_g"],
                      p["w_out"], p["ln1_g"], p["ln1_b"], p["w_gate"], p["w_up"], p["w_down"],
                      p["ln2_g"], p["ln2_b"], alpha=alpha)
    return y.reshape(bsz, t_len, D_MODEL), s_gla_new, s_gdn_new, buf_new


def kernel(x_prompt, x_sample, state_gla, state_gdn, state_gdn_conv, w_in, gla_w_gate_up, gla_b_gate,
           gla_norm_g, gdn_conv_w, gdn_a_log, gdn_dt_bias, gdn_norm_g, w_out, ln1_g, ln1_b,
           w_ffn_gate, w_ffn_up, w_ffn_down, ln2_g, ln2_b):
    depth = w_in.shape[0]
    alpha = float((2 * depth) ** 0.25)
    bp = x_prompt.shape[0]
    yp, ys = x_prompt, x_sample
    outs = [[] for _ in range(6)]
    for l in range(depth):
        w_gla, w_gdn, w_small = _regroup_w_in(w_in[l])
        wup_pad = jnp.zeros((SMALL_COLS, GLA_QK), F32).at[0:GLA_GATE_RANK].set(
            gla_w_gate_up[l].astype(F32)).astype(BF16)
        p = dict(w_gla=w_gla, w_gdn=w_gdn, w_small=w_small, wup_pad=wup_pad,
                 b_gate=gla_b_gate[l].astype(F32)[None, :], gla_norm_g=gla_norm_g[l].astype(F32)[None, :],
                 conv_w=gdn_conv_w[l].astype(F32), a_log=gdn_a_log[l], dt_bias=gdn_dt_bias[l],
                 gdn_norm_g=gdn_norm_g[l].astype(F32)[None, :], w_out=w_out[l].astype(BF16),
                 ln1_g=ln1_g[l][None, :], ln1_b=ln1_b[l][None, :], w_gate=w_ffn_gate[l].astype(BF16),
                 w_up=w_ffn_up[l].astype(BF16), w_down=w_ffn_down[l].astype(BF16),
                 ln2_g=ln2_g[l][None, :], ln2_b=ln2_b[l][None, :])
        z_gla = jnp.zeros((bp,) + state_gla.shape[2:], state_gla.dtype)
        z_gdn = jnp.zeros((bp,) + state_gdn.shape[2:], state_gdn.dtype)
        z_conv = jnp.zeros((bp,) + state_gdn_conv.shape[2:], state_gdn_conv.dtype)
        yp, a1, a2, a3 = _trunk_layer(yp, z_gla, z_gdn, z_conv, p, alpha=alpha)
        ys, b1, b2, b3 = _trunk_layer(ys, state_gla[l], state_gdn[l], state_gdn_conv[l], p, alpha=alpha)
        for lst, val in zip(outs, (a1, a2, a3, b1, b2, b3)):
            lst.append(val)
    return (yp, ys) + tuple(jnp.stack(o) for o in outs)
```
